```python
import jax, jax.numpy as jnp
from jax import lax
import numpy as np

D_MODEL = 2048
BATCH = 4
SEQ = 8192
DEPTH = 4

CHUNK = 64
N_MIXERS = 2
N_LAYERS_A = (DEPTH + 1) // 2
N_LAYERS_B = DEPTH // 2

GM_BLOCK = 128
GM_GROUPS = 8
GM_WIDTH = D_MODEL
GM_GROUP_DIM = GM_WIDTH // GM_GROUPS

GLA_HEADS = 4
GLA_KEY_DIM = D_MODEL // 2
GLA_VAL_DIM = D_MODEL
GLA_HEAD_K = GLA_KEY_DIM // GLA_HEADS
GLA_HEAD_V = GLA_VAL_DIM // GLA_HEADS
GLA_GATE_RANK = 16
GLA_GATE_TAU = 16.0

FFN_DIM = 4 * D_MODEL

EPS = 1e-6

kernel_name = "hybrid_gmlp_gla_sqrelu_streaming_encoder"


def rms_norm(x, g):
    xf = x.astype(jnp.float32)
    y = xf * lax.rsqrt(jnp.mean(xf * xf, axis=-1, keepdims=True) + EPS)
    return (y * g.astype(jnp.float32)).astype(x.dtype)


def layer_norm(x, g, b):
    xf = x.astype(jnp.float32)
    mu = jnp.mean(xf, axis=-1, keepdims=True)
    xc = xf - mu
    y = xc * lax.rsqrt(jnp.mean(xc * xc, axis=-1, keepdims=True) + EPS)
    return (y * g.astype(jnp.float32) + b.astype(jnp.float32)).astype(x.dtype)


def gmlp_mixer(h, w_in, ln_g, ln_b, w_s, b_s, w_out):
    bsz, seq, _ = h.shape
    z = jax.nn.gelu(h @ w_in, approximate=False)
    u, v = jnp.split(z, 2, axis=-1)
    v = layer_norm(v, ln_g, ln_b)
    v = v.reshape(bsz, seq // GM_BLOCK, GM_BLOCK, GM_GROUPS, GM_GROUP_DIM)
    chunk_id = jnp.arange(GM_BLOCK) // CHUNK
    mask = chunk_id[None, :] <= chunk_id[:, None]
    w = jnp.where(mask[None], w_s, jnp.zeros((), w_s.dtype))
    mixed = jnp.einsum('gij,bnjgc->bnigc', w, v) + b_s.T[None, None, :, :, None]
    gated = u * mixed.reshape(bsz, seq, GM_WIDTH)
    return gated @ w_out


def gla_mixer(h, w_in, w_a1, w_a2, b_a, norm_g, w_o):
    bsz, seq, _ = h.shape
    n_chunks = seq // CHUNK
    f32 = jnp.float32
    proj = h @ w_in
    q, k, v, r = jnp.split(proj, [GLA_KEY_DIM, 2 * GLA_KEY_DIM, 2 * GLA_KEY_DIM + GLA_VAL_DIM], axis=-1)
    logit = ((h @ w_a1) @ w_a2 + b_a).astype(f32)
    log_alpha = jax.nn.log_sigmoid(logit) / GLA_GATE_TAU

    def to_chunks(t, dh):
        return t.astype(f32).reshape(bsz, n_chunks, CHUNK, GLA_HEADS, dh).transpose(1, 0, 3, 2, 4)

    qc = to_chunks(q, GLA_HEAD_K) * (GLA_HEAD_K ** -0.5)
    kc = to_chunks(k, GLA_HEAD_K)
    vc = to_chunks(v, GLA_HEAD_V)
    gc = lax.cumsum(to_chunks(log_alpha, GLA_HEAD_K), axis=3)

    def step(state, inp):
        q_t, k_t, v_t, g_t = inp
        g_end = g_t[:, :, -1:, :]
        k_dec = k_t * jnp.exp(g_end - g_t)
        state = jnp.exp(g_end[:, :, 0, :])[..., None] * state + jnp.einsum('bhck,bhcv->bhkv', k_dec, v_t)
        out = jnp.einsum('bhck,bhkv->bhcv', q_t, state)
        return state, out

    s0 = jnp.zeros((bsz, GLA_HEADS, GLA_HEAD_K, GLA_HEAD_V), f32)
    _, o = lax.scan(step, s0, (qc, kc, vc, gc))
    o = o.transpose(1, 0, 3, 2, 4).reshape(bsz, seq, GLA_HEADS, GLA_HEAD_V)
    o = o * lax.rsqrt(jnp.mean(o * o, axis=-1, keepdims=True) + EPS)
    o = o.reshape(bsz, seq, GLA_VAL_DIM) * norm_g.astype(f32)
    o = o.astype(h.dtype) * jax.nn.silu(r)
    return o @ w_o


def squared_relu_mlp(h, w_up, w_down):
    a = jax.nn.relu(h @ w_up)
    return (a * a) @ w_down


def setup_inputs(seed: int = 0) -> dict:
    key = jax.random.key(seed)
    ks = jax.random.split(key, 20)
    f32 = jnp.float32

    def nrm(k, shape, fan_in):
        return jax.random.normal(k, shape, f32) * (fan_in ** -0.5)

    def gain(k, shape):
        return 1.0 + 0.02 * jax.random.normal(k, shape, f32)

    def small(k, shape):
        return 0.01 * jax.random.normal(k, shape, f32)

    return {
        "x": jax.random.normal(ks[0], (BATCH, SEQ, D_MODEL), f32),
        "norm_mix_g": gain(ks[1], (DEPTH, D_MODEL)),
        "norm_ffn_g": gain(ks[2], (DEPTH, D_MODEL)),
        "final_g": gain(ks[3], (D_MODEL,)),
        "gm_w_in": nrm(ks[4], (N_LAYERS_A, D_MODEL, 2 * GM_WIDTH), D_MODEL),
        "gm_ln_g": gain(ks[5], (N_LAYERS_A, GM_WIDTH)),
        "gm_ln_b": small(ks[6], (N_LAYERS_A, GM_WIDTH)),
        "gm_w_s": nrm(ks[7], (N_LAYERS_A, GM_GROUPS, GM_BLOCK, GM_BLOCK), GM_BLOCK),
        "gm_b_s": 1.0 + small(ks[8], (N_LAYERS_A, GM_GROUPS, GM_BLOCK)),
        "gm_w_out": nrm(ks[9], (N_LAYERS_A, GM_WIDTH, D_MODEL), GM_WIDTH),
        "gla_w_in": nrm(ks[10], (N_LAYERS_B, D_MODEL, 2 * GLA_KEY_DIM + 2 * GLA_VAL_DIM), D_MODEL),
        "gla_w_a1": nrm(ks[11], (N_LAYERS_B, D_MODEL, GLA_GATE_RANK), D_MODEL),
        "gla_w_a2": nrm(ks[12], (N_LAYERS_B, GLA_GATE_RANK, GLA_KEY_DIM), GLA_GATE_RANK),
        "gla_b_a": small(ks[13], (N_LAYERS_B, GLA_KEY_DIM)),
        "gla_norm_g": gain(ks[14], (N_LAYERS_B, GLA_VAL_DIM)),
        "gla_w_o": nrm(ks[15], (N_LAYERS_B, GLA_VAL_DIM, D_MODEL), GLA_VAL_DIM),
        "ffn_w_up": nrm(ks[16], (DEPTH, D_MODEL, FFN_DIM), D_MODEL),
        "ffn_w_down": nrm(ks[17], (DEPTH, FFN_DIM, D_MODEL), FFN_DIM),
    }


def reference(x, norm_mix_g, norm_ffn_g, final_g, gm_w_in, gm_ln_g, gm_ln_b, gm_w_s, gm_b_s, gm_w_out,
              gla_w_in, gla_w_a1, gla_w_a2, gla_b_a, gla_norm_g, gla_w_o, ffn_w_up, ffn_w_down):
    for i in range(DEPTH):
        h = rms_norm(x, norm_mix_g[i])
        j = i // N_MIXERS
        if i % N_MIXERS == 0:
            x = x + gmlp_mixer(h, gm_w_in[j], gm_ln_g[j], gm_ln_b[j], gm_w_s[j], gm_b_s[j], gm_w_out[j])
        else:
            x = x + gla_mixer(h, gla_w_in[j], gla_w_a1[j], gla_w_a2[j], gla_b_a[j], gla_norm_g[j], gla_w_o[j])
        h = rms_norm(x, norm_ffn_g[i])
        x = x + squared_relu_mlp(h, ffn_w_up[i], ffn_w_down[i])
    return rms_norm(x, final_g)
```

```python
import functools

import jax
import jax.numpy as jnp
from jax import lax
from jax.experimental import pallas as pl
from jax.experimental.pallas import tpu as pltpu

CHUNK = 64
GM_BLOCK = 128
GM_GROUPS = 8
GLA_HEADS = 4
GLA_GATE_TAU = 16.0
EPS = 1e-6

LANES = 128
V7X_VMEM_LIMIT_BYTES = 56 * 1024 * 1024

F32 = jnp.float32
BF16 = jnp.bfloat16


def _params(semantics):
    return pltpu.CompilerParams(dimension_semantics=semantics, vmem_limit_bytes=V7X_VMEM_LIMIT_BYTES)


def _tile(n, want):
    t = min(n, want)
    assert n % t == 0, (n, t)
    return t


def _rms_norm(x, g):
    return x * lax.rsqrt(jnp.mean(x * x, axis=-1, keepdims=True) + EPS) * g


def _gelu_exact(z):
    return 0.5 * z * (1.0 + lax.erf(z * 0.7071067811865476))


def _log_sigmoid(x):
    return jnp.minimum(x, 0.0) - jnp.log1p(jnp.exp(-jnp.abs(x)))


def _gm_in_kernel(x_ref, g_ref, w_ref, z_ref, hn_ref):
    @pl.when(pl.program_id(1) == 0)
    def _():
        hn_ref[...] = _rms_norm(x_ref[...], g_ref[...]).astype(BF16)

    z = jnp.dot(hn_ref[...], w_ref[...], preferred_element_type=F32)
    z_ref[...] = _gelu_exact(z).astype(z_ref.dtype)


def _gm_in(x, g, w):
    m, d = x.shape
    n = w.shape[1]
    tm, tn = _tile(m, 1024), _tile(n, 1024)
    return pl.pallas_call(
        _gm_in_kernel,
        grid=(m // tm, n // tn),
        in_specs=[
            pl.BlockSpec((tm, d), lambda i, j: (i, 0)),
            pl.BlockSpec((1, d), lambda i, j: (0, 0)),
            pl.BlockSpec((d, tn), lambda i, j: (0, j)),
        ],
        out_specs=pl.BlockSpec((tm, tn), lambda i, j: (i, j)),
        out_shape=jax.ShapeDtypeStruct((m, n), BF16),
        scratch_shapes=[pltpu.VMEM((tm, d), BF16)],
        compiler_params=_params(("parallel", "arbitrary")),
        name="gm_in",
    )(x, g, w)


def _gm_spatial_kernel(u_ref, v_ref, lng_ref, lnb_ref, ws_ref, bst_ref, o_ref, *, n_blocks, group_dim):
    v = v_ref[...].astype(F32)
    mu = jnp.mean(v, axis=-1, keepdims=True)
    vc = v - mu
    vn = vc * lax.rsqrt(jnp.mean(vc * vc, axis=-1, keepdims=True) + EPS) * lng_ref[...] + lnb_ref[...]
    vn = vn.astype(BF16)

    row_chunk = lax.broadcasted_iota(jnp.int32, (GM_BLOCK, GM_BLOCK), 0) // CHUNK
    col_chunk = lax.broadcasted_iota(jnp.int32, (GM_BLOCK, GM_BLOCK), 1) // CHUNK
    causal = col_chunk <= row_chunk
    bst = bst_ref[...]
    for g in range(GM_GROUPS):
        w = jnp.where(causal, ws_ref[g], 0.0).astype(BF16)
        bias = bst[:, g:g + 1]
        cols = slice(g * group_dim, (g + 1) * group_dim)
        for nb in range(n_blocks):
            rows = slice(nb * GM_BLOCK, (nb + 1) * GM_BLOCK)
            mixed = jnp.dot(w, vn[rows, cols], preferred_element_type=F32) + bias
            o_ref[rows, cols] = (u_ref[rows, cols].astype(F32) * mixed).astype(o_ref.dtype)


def _gm_spatial(z, ln_g, ln_b, w_s, b_s_t):
    m, two_w = z.shape
    width = two_w // 2
    tm = _tile(m, 512)
    assert tm % GM_BLOCK == 0 and width % (GM_GROUPS * LANES) == 0
    kern = functools.partial(_gm_spatial_kernel, n_blocks=tm // GM_BLOCK, group_dim=width // GM_GROUPS)
    return pl.pallas_call(
        kern,
        grid=(m // tm,),
        in_specs=[
            pl.BlockSpec((tm, width), lambda i: (i, 0)),
            pl.BlockSpec((tm, width), lambda i: (i, 1)),
            pl.BlockSpec((1, width), lambda i: (0, 0)),
            pl.BlockSpec((1, width), lambda i: (0, 0)),
            pl.BlockSpec((GM_GROUPS, GM_BLOCK, GM_BLOCK), lambda i: (0, 0, 0)),
            pl.BlockSpec((GM_BLOCK, GM_GROUPS), lambda i: (0, 0)),
        ],
        out_specs=pl.BlockSpec((tm, width), lambda i: (i, 0)),
        out_shape=jax.ShapeDtypeStruct((m, width), BF16),
        compiler_params=_params(("parallel",)),
        name="gm_spatial",
    )(z, z, ln_g, ln_b, w_s, b_s_t)


def _mm_residual_kernel(a_ref, w_ref, x_ref, o_ref):
    o_ref[...] = x_ref[...] + jnp.dot(a_ref[...], w_ref[...], preferred_element_type=F32)


def _mm_residual(a, w, x):
    m, k = a.shape
    n = w.shape[1]
    tm, tn = _tile(m, 1024), _tile(n, 1024)
    return pl.pallas_call(
        _mm_residual_kernel,
        grid=(m // tm, n // tn),
        in_specs=[
            pl.BlockSpec((tm, k), lambda i, j: (i, 0)),
            pl.BlockSpec((k, tn), lambda i, j: (0, j)),
            pl.BlockSpec((tm, tn), lambda i, j: (i, j)),
        ],
        out_specs=pl.BlockSpec((tm, tn), lambda i, j: (i, j)),
        out_shape=jax.ShapeDtypeStruct((m, n), F32),
        compiler_params=_params(("parallel", "arbitrary")),
        name="mm_residual",
    )(a, w, x)


def _gla_in_kernel(x_ref, g_ref, w_ref, wa1_ref, wa2_ref, ba_ref, p_ref, dec_ref, hn_ref, e_ref, *, k_tile):
    j = pl.program_id(1)

    @pl.when(j == 0)
    def _():
        hn = _rms_norm(x_ref[...], g_ref[...]).astype(BF16)
        hn_ref[...] = hn
        low = jnp.dot(hn, wa1_ref[...], preferred_element_type=F32)
        logit = jnp.dot(low.astype(BF16), wa2_ref[...], preferred_element_type=F32) + ba_ref[...]
        gc = _log_sigmoid(logit) * (1.0 / GLA_GATE_TAU)
        tm, kd = gc.shape
        pos = lax.broadcasted_iota(jnp.int32, (tm, kd), 0) % CHUNK
        shift = 1
        while shift < CHUNK:
            gc = gc + jnp.where(pos >= shift, pltpu.roll(gc, shift, axis=0), 0.0)
            shift *= 2
        gc3 = gc.reshape(tm // CHUNK, CHUNK, kd)
        g_end = gc3[:, CHUNK - 1:CHUNK, :]
        e_ref[...] = jnp.exp(g_end - gc3).reshape(tm, kd)
        dec_ref[...] = jnp.exp(g_end).reshape(tm // CHUNK, kd)

    proj = jnp.dot(hn_ref[...], w_ref[...], preferred_element_type=F32)

    @pl.when(j == k_tile)
    def _():
        p_ref[...] = (proj * e_ref[...]).astype(p_ref.dtype)

    @pl.when(j != k_tile)
    def _():
        p_ref[...] = proj.astype(p_ref.dtype)


def _gla_in(x, g, w, wa1, wa2, ba):
    m, d = x.shape
    n = w.shape[1]
    kd = wa2.shape[1]
    tm = _tile(m, 1024)
    tn = kd
    assert n % tn == 0 and tm % CHUNK == 0 and (tm // CHUNK) % 8 == 0
    rank_pad = wa1.shape[1]
    kern = functools.partial(_gla_in_kernel, k_tile=1)
    return pl.pallas_call(
        kern,
        grid=(m // tm, n // tn),
        in_specs=[
            pl.BlockSpec((tm, d), lambda i, j: (i, 0)),
            pl.BlockSpec((1, d), lambda i, j: (0, 0)),
            pl.BlockSpec((d, tn), lambda i, j: (0, j)),
            pl.BlockSpec((d, rank_pad), lambda i, j: (0, 0)),
            pl.BlockSpec((rank_pad, kd), lambda i, j: (0, 0)),
            pl.BlockSpec((1, kd), lambda i, j: (0, 0)),
        ],
        out_specs=[
            pl.BlockSpec((tm, tn), lambda i, j: (i, j)),
            pl.BlockSpec((tm // CHUNK, kd), lambda i, j: (i, 0)),
        ],
        out_shape=[
            jax.ShapeDtypeStruct((m, n), BF16),
            jax.ShapeDtypeStruct((m // CHUNK, kd), F32),
        ],
        scratch_shapes=[pltpu.VMEM((tm, d), BF16), pltpu.VMEM((tm, kd), F32)],
        compiler_params=_params(("parallel", "arbitrary")),
        name="gla_in",
    )(x, g, w, wa1, wa2, ba)


def _gla_scan_kernel(q_ref, k_ref, v_ref, r_ref, dec_ref, ng_ref, o_ref, s_ref, *, n_chunks, q_scale):
    @pl.when(pl.program_id(2) == 0)
    def _():
        s_ref[...] = jnp.zeros_like(s_ref)

    dec = dec_ref[0]
    ng = ng_ref[...]
    for c in range(n_chunks):
        rows = slice(c * CHUNK, (c + 1) * CHUNK)
        kv = lax.dot_general(k_ref[rows, :], v_ref[rows, :], (((0,), (0,)), ((), ())),
                             preferred_element_type=F32)
        s = dec[:, c:c + 1] * s_ref[...] + kv
        s_ref[...] = s
        o = jnp.dot(q_ref[rows, :], s.astype(BF16), preferred_element_type=F32) * q_scale
        o = o * lax.rsqrt(jnp.mean(o * o, axis=-1, keepdims=True) + EPS) * ng
        r = r_ref[rows, :].astype(F32)
        o_ref[rows, :] = (o * (r * jax.nn.sigmoid(r))).astype(o_ref.dtype)


def _gla_scan(proj, dec_t, norm_g, batch, seq, kd, vd):
    m = proj.shape[0]
    hk, hv = kd // GLA_HEADS, vd // GLA_HEADS
    ts = _tile(seq, 512)
    n_chunks = ts // CHUNK
    steps = seq // ts
    assert hk % LANES == 0 and hv % LANES == 0 and hv % hk == 0
    v_blk0 = 2 * kd // hv
    r_blk0 = (2 * kd + vd) // hv
    kern = functools.partial(_gla_scan_kernel, n_chunks=n_chunks, q_scale=float(hk) ** -0.5)
    return pl.pallas_call(
        kern,
        grid=(batch, GLA_HEADS, steps),
        in_specs=[
            pl.BlockSpec((ts, hk), lambda b, h, s: (b * steps + s, h)),
            pl.BlockSpec((ts, hk), lambda b, h, s: (b * steps + s, GLA_HEADS + h)),
            pl.BlockSpec((ts, hv), lambda b, h, s: (b * steps + s, v_blk0 + h)),
            pl.BlockSpec((ts, hv), lambda b, h, s: (b * steps + s, r_blk0 + h)),
            pl.BlockSpec((1, hk, n_chunks), lambda b, h, s: (b * steps + s, h, 0)),
            pl.BlockSpec((1, hv), lambda b, h, s: (0, h)),
        ],
        out_specs=pl.BlockSpec((ts, hv), lambda b, h, s: (b * steps + s, h)),
        out_shape=jax.ShapeDtypeStruct((m, vd), BF16),
        scratch_shapes=[pltpu.VMEM((hk, hv), F32)],
        compiler_params=_params(("parallel", "parallel", "arbitrary")),
        name="gla_scan",
    )(proj, proj, proj, proj, dec_t, norm_g)


def _ffn_kernel(x_ref, g_ref, wu_ref, wd_ref, fg_ref, o_ref, hn_ref, *, final_norm):
    j = pl.program_id(1)

    @pl.when(j == 0)
    def _():
        x = x_ref[...]
        hn_ref[...] = _rms_norm(x, g_ref[...]).astype(BF16)
        o_ref[...] = x

    a = jnp.maximum(jnp.dot(hn_ref[...], wu_ref[...], preferred_element_type=F32), 0.0)
    o_ref[...] += jnp.dot((a * a).astype(BF16), wd_ref[...], preferred_element_type=F32)

    if final_norm:
        @pl.when(j == pl.num_programs(1) - 1)
        def _():
            o_ref[...] = _rms_norm(o_ref[...], fg_ref[...])


def _ffn(x, g, w_up, w_down, final_g, final_norm):
    m, d = x.shape
    f = w_up.shape[1]
    tm, tf = _tile(m, 1024), _tile(f, 512)
    kern = functools.partial(_ffn_kernel, final_norm=final_norm)
    return pl.pallas_call(
        kern,
        grid=(m // tm, f // tf),
        in_specs=[
            pl.BlockSpec((tm, d), lambda i, j: (i, 0)),
            pl.BlockSpec((1, d), lambda i, j: (0, 0)),
            pl.BlockSpec((d, tf), lambda i, j: (0, j)),
            pl.BlockSpec((tf, d), lambda i, j: (j, 0)),
            pl.BlockSpec((1, d), lambda i, j: (0, 0)),
        ],
        out_specs=pl.BlockSpec((tm, d), lambda i, j: (i, 0)),
        out_shape=jax.ShapeDtypeStruct((m, d), F32),
        scratch_shapes=[pltpu.VMEM((tm, d), BF16)],
        compiler_params=_params(("parallel", "arbitrary")),
        name="ffn",
    )(x, g, w_up, w_down, final_g)


def kernel(x, norm_mix_g, norm_ffn_g, final_g, gm_w_in, gm_ln_g, gm_ln_b, gm_w_s, gm_b_s, gm_w_out,
           gla_w_in, gla_w_a1, gla_w_a2, gla_b_a, gla_norm_g, gla_w_o, ffn_w_up, ffn_w_down):
    batch, seq, d = x.shape
    depth = norm_mix_g.shape[0]
    m = batch * seq
    h = x.reshape(m, d)
    final_g2 = final_g.reshape(1, d)

    for i in range(depth):
        jdx = i // 2
        mix_g = norm_mix_g[i].reshape(1, d)
        if i % 2 == 0:
            z = _gm_in(h, mix_g, gm_w_in[jdx].astype(BF16))
            gated = _gm_spatial(z, gm_ln_g[jdx].reshape(1, -1), gm_ln_b[jdx].reshape(1, -1),
                                gm_w_s[jdx], gm_b_s[jdx].T)
            h = _mm_residual(gated, gm_w_out[jdx].astype(BF16), h)
        else:
            kd = gla_w_a2.shape[2]
            vd = gla_w_o.shape[1]
            rank = gla_w_a1.shape[2]
            wa1 = jnp.pad(gla_w_a1[jdx], ((0, 0), (0, LANES - rank))).astype(BF16)
            wa2 = jnp.pad(gla_w_a2[jdx], ((0, LANES - rank), (0, 0))).astype(BF16)
            proj, dec = _gla_in(h, mix_g, gla_w_in[jdx].astype(BF16), wa1, wa2,
                                gla_b_a[jdx].reshape(1, kd))
            ts = _tile(seq, 512)
            dec_t = dec.reshape(m // ts, ts // CHUNK, kd).transpose(0, 2, 1)
            o = _gla_scan(proj, dec_t, gla_norm_g[jdx].reshape(1, vd), batch, seq, kd, vd)
            h = _mm_residual(o, gla_w_o[jdx].astype(BF16), h)
        h = _ffn(h, norm_ffn_g[i].reshape(1, d), ffn_w_up[i].astype(BF16), ffn_w_down[i].astype(BF16),
                 final_g2, final_norm=(i == depth - 1))
    return h.reshape(batch, seq, d)
```

```python
import functools

import jax
import jax.numpy as jnp
from jax import lax
from jax.experimental import pallas as pl
from jax.experimental.pallas import tpu as pltpu

CHUNK = 64
GM_BLOCK = 128
GM_GROUPS = 8
GLA_HEADS = 4
GLA_GATE_TAU = 16.0
EPS = 1e-6

LANES = 128
V7X_VMEM_LIMIT_BYTES = 56 * 1024 * 1024

F32 = jnp.float32
BF16 = jnp.bfloat16


def _params(semantics):
    return pltpu.CompilerParams(dimension_semantics=semantics, vmem_limit_bytes=V7X_VMEM_LIMIT_BYTES)


def _tile(n, want):
    t = min(n, want)
    assert n % t == 0, (n, t)
    return t


def _rms_norm(x, g):
    return x * lax.rsqrt(jnp.mean(x * x, axis=-1, keepdims=True) + EPS) * g


def _gelu_exact(z):
    return 0.5 * z * (1.0 + lax.erf(z * 0.7071067811865476))


def _log_sigmoid(x):
    return jnp.minimum(x, 0.0) - jnp.log(1.0 + jnp.exp(-jnp.abs(x)))


def _gm_in_kernel(x_ref, g_ref, w_ref, z_ref, hn_ref):
    def project(hn):
        z = jnp.dot(hn, w_ref[...], preferred_element_type=F32)
        z_ref[...] = _gelu_exact(z).astype(z_ref.dtype)

    @pl.when(pl.program_id(1) == 0)
    def _():
        hn = _rms_norm(x_ref[...], g_ref[...]).astype(BF16)
        hn_ref[...] = hn
        project(hn)

    @pl.when(pl.program_id(1) != 0)
    def _():
        project(hn_ref[...])


def _gm_in(x, g, w):
    m, d = x.shape
    n = w.shape[1]
    tm, tn = _tile(m, 1024), _tile(n, 1024)
    return pl.pallas_call(
        _gm_in_kernel,
        grid=(m // tm, n // tn),
        in_specs=[
            pl.BlockSpec((tm, d), lambda i, j: (i, 0)),
            pl.BlockSpec((1, d), lambda i, j: (0, 0)),
            pl.BlockSpec((d, tn), lambda i, j: (0, j)),
        ],
        out_specs=pl.BlockSpec((tm, tn), lambda i, j: (i, j)),
        out_shape=jax.ShapeDtypeStruct((m, n), BF16),
        scratch_shapes=[pltpu.VMEM((tm, d), BF16)],
        compiler_params=_params(("parallel", "arbitrary")),
        name="gm_in",
    )(x, g, w)


def _gm_spatial_kernel(u_ref, v_ref, lng_ref, lnb_ref, ws_ref, bst_ref, o_ref, *, n_blocks, group_dim):
    v = v_ref[...].astype(F32)
    mu = jnp.mean(v, axis=-1, keepdims=True)
    vc = v - mu
    vn = vc * lax.rsqrt(jnp.mean(vc * vc, axis=-1, keepdims=True) + EPS) * lng_ref[...] + lnb_ref[...]
    vn = vn.astype(BF16)

    row_chunk = lax.broadcasted_iota(jnp.int32, (GM_BLOCK, GM_BLOCK), 0) // CHUNK
    col_chunk = lax.broadcasted_iota(jnp.int32, (GM_BLOCK, GM_BLOCK), 1) // CHUNK
    causal = col_chunk <= row_chunk
    bst = bst_ref[...]
    for g in range(GM_GROUPS):
        w = jnp.where(causal, ws_ref[g], 0.0).astype(BF16)
        bias = bst[:, g:g + 1]
        cols = slice(g * group_dim, (g + 1) * group_dim)
        for nb in range(n_blocks):
            rows = slice(nb * GM_BLOCK, (nb + 1) * GM_BLOCK)
            mixed = jnp.dot(w, vn[rows, cols], preferred_element_type=F32) + bias
            o_ref[rows, cols] = (u_ref[rows, cols].astype(F32) * mixed).astype(o_ref.dtype)


def _gm_spatial(z, ln_g, ln_b, w_s, b_s_t):
    m, two_w = z.shape
    width = two_w // 2
    tm = _tile(m, 512)
    assert tm % GM_BLOCK == 0 and width % (GM_GROUPS * LANES) == 0
    kern = functools.partial(_gm_spatial_kernel, n_blocks=tm // GM_BLOCK, group_dim=width // GM_GROUPS)
    return pl.pallas_call(
        kern,
        grid=(m // tm,),
        in_specs=[
            pl.BlockSpec((tm, width), lambda i: (i, 0)),
            pl.BlockSpec((tm, width), lambda i: (i, 1)),
            pl.BlockSpec((1, width), lambda i: (0, 0)),
            pl.BlockSpec((1, width), lambda i: (0, 0)),
            pl.BlockSpec((GM_GROUPS, GM_BLOCK, GM_BLOCK), lambda i: (0, 0, 0)),
            pl.BlockSpec((GM_BLOCK, GM_GROUPS), lambda i: (0, 0)),
        ],
        out_specs=pl.BlockSpec((tm, width), lambda i: (i, 0)),
        out_shape=jax.ShapeDtypeStruct((m, width), BF16),
        compiler_params=_params(("parallel",)),
        name="gm_spatial",
    )(z, z, ln_g, ln_b, w_s, b_s_t)


def _mm_residual_kernel(a_ref, w_ref, x_ref, o_ref):
    o_ref[...] = x_ref[...] + jnp.dot(a_ref[...], w_ref[...], preferred_element_type=F32)


def _mm_residual(a, w, x):
    m, k = a.shape
    n = w.shape[1]
    tm, tn = _tile(m, 1024), _tile(n, 1024)
    return pl.pallas_call(
        _mm_residual_kernel,
        grid=(m // tm, n // tn),
        in_specs=[
            pl.BlockSpec((tm, k), lambda i, j: (i, 0)),
            pl.BlockSpec((k, tn), lambda i, j: (0, j)),
            pl.BlockSpec((tm, tn), lambda i, j: (i, j)),
        ],
        out_specs=pl.BlockSpec((tm, tn), lambda i, j: (i, j)),
        out_shape=jax.ShapeDtypeStruct((m, n), F32),
        compiler_params=_params(("parallel", "arbitrary")),
        name="mm_residual",
    )(a, w, x)


def _gla_w_tile(j, n_tiles):
    return jnp.where(j == 0, 0, jnp.where(j == n_tiles - 1, 1, j + 1))


def _gla_p_tile(j, n_tiles):
    return jnp.where(j == 0, n_tiles - 2, jnp.minimum(j - 1, n_tiles - 3))


def _gla_in_kernel(x_ref, g_ref, w_ref, wa1_ref, wa2_ref, ba_ref, p_ref, kt_ref, dec_ref, hn_ref, gs_ref, *,
                   n_vtiles):
    j = pl.program_id(1)
    last = pl.num_programs(1) - 1

    def project(hn):
        return jnp.dot(hn, w_ref[...], preferred_element_type=F32)

    @pl.when(j == 0)
    def _():
        hn = _rms_norm(x_ref[...], g_ref[...]).astype(BF16)
        hn_ref[...] = hn
        p_ref[...] = project(hn).astype(p_ref.dtype)
        low = jnp.dot(hn, wa1_ref[...], preferred_element_type=F32)
        gs_ref[...] = jnp.dot(low.astype(BF16), wa2_ref[...], preferred_element_type=F32) + ba_ref[...]

    tm, kd = gs_ref.shape
    tn = p_ref.shape[1]
    n_slabs = 4
    rs, cs = tm // n_slabs, tn // n_slabs

    def project_slab(s):
        cols = slice(s * cs, (s + 1) * cs)
        p_ref[:, cols] = jnp.dot(hn_ref[...], w_ref[:, cols], preferred_element_type=F32).astype(p_ref.dtype)

    def log_gate_slab(s):
        rows = slice(s * rs, (s + 1) * rs)
        gs_ref[rows, :] = _log_sigmoid(gs_ref[rows, :]) * (1.0 / GLA_GATE_TAU)

    def decay_slab(s):
        rows = slice(s * rs, (s + 1) * rs)
        gc = gs_ref[rows, :]
        pos = lax.broadcasted_iota(jnp.int32, (rs, kd), 0) % CHUNK
        shift = 1
        while shift < CHUNK:
            gc = gc + jnp.where(pos >= shift, pltpu.roll(gc, shift, axis=0), 0.0)
            shift *= 2
        gc3 = gc.reshape(rs // CHUNK, CHUNK, kd)
        g_end = gc3[:, CHUNK - 1:CHUNK, :]
        gs_ref[rows, :] = jnp.exp(g_end - gc3).reshape(rs, kd)
        dec_ref[s * (rs // CHUNK):(s + 1) * (rs // CHUNK), :] = jnp.exp(g_end).reshape(rs // CHUNK, kd)

    @pl.when(j == 1)
    def _():
        for s in range(n_slabs):
            log_gate_slab(s)
            project_slab(s)

    @pl.when(j == 2)
    def _():
        for s in range(n_slabs):
            decay_slab(s)
            project_slab(s)

    if n_vtiles > 2:
        @pl.when((j > 2) & (j <= n_vtiles))
        def _():
            p_ref[...] = project(hn_ref[...]).astype(p_ref.dtype)

    @pl.when((j > n_vtiles) & (j < last))
    def _():
        r = project(hn_ref[...])
        p_ref[...] = (r * jax.nn.sigmoid(r)).astype(p_ref.dtype)

    @pl.when(j == last)
    def _():
        kt_ref[...] = (project(hn_ref[...]) * gs_ref[...]).T.astype(kt_ref.dtype)


def _gla_in(x, g, w, wa1, wa2, ba, vd):
    m, d = x.shape
    n = w.shape[1]
    kd = wa2.shape[1]
    tm = _tile(m, 1024)
    tn = kd
    n_tiles = n // tn
    n_vtiles = vd // tn
    assert n == 2 * kd + 2 * vd and vd % tn == 0 and n_vtiles >= 2
    assert tm % CHUNK == 0 and (tm // CHUNK) % 8 == 0 and tm % LANES == 0
    rank_pad = wa1.shape[1]
    kern = functools.partial(_gla_in_kernel, n_vtiles=n_vtiles)
    return pl.pallas_call(
        kern,
        grid=(m // tm, n_tiles),
        in_specs=[
            pl.BlockSpec((tm, d), lambda i, j: (i, 0)),
            pl.BlockSpec((1, d), lambda i, j: (0, 0)),
            pl.BlockSpec((d, tn), lambda i, j: (0, _gla_w_tile(j, n_tiles))),
            pl.BlockSpec((d, rank_pad), lambda i, j: (0, 0)),
            pl.BlockSpec((rank_pad, kd), lambda i, j: (0, 0)),
            pl.BlockSpec((1, kd), lambda i, j: (0, 0)),
        ],
        out_specs=[
            pl.BlockSpec((tm, tn), lambda i, j: (i, _gla_p_tile(j, n_tiles))),
            pl.BlockSpec((kd, tm), lambda i, j: (0, i)),
            pl.BlockSpec((tm // CHUNK, kd), lambda i, j: (i, 0)),
        ],
        out_shape=[
            jax.ShapeDtypeStruct((m, n - kd), BF16),
            jax.ShapeDtypeStruct((kd, m), BF16),
            jax.ShapeDtypeStruct((m // CHUNK, kd), F32),
        ],
        scratch_shapes=[pltpu.VMEM((tm, d), BF16), pltpu.VMEM((tm, kd), F32)],
        compiler_params=_params(("parallel", "arbitrary")),
        name="gla_in",
    )(x, g, w, wa1, wa2, ba)


def _gla_scan_kernel(q_ref, kt_ref, v_ref, sr_ref, dec_ref, ng_ref, o_ref, s_ref, sb_ref, *, n_chunks, hk, hv):
    @pl.when(pl.program_id(1) == 0)
    def _():
        s_ref[...] = jnp.zeros_like(s_ref)

    ng = ng_ref[...]
    norm_eps = EPS * hk
    pair = 2 * CHUNK
    first_half = lax.broadcasted_iota(jnp.int32, (hk, pair), 1) < CHUNK
    heads = range(GLA_HEADS)
    kcols = [slice(h * hk, (h + 1) * hk) for h in heads]
    vcols = [slice(h * hv, (h + 1) * hv) for h in heads]

    def advance_state(c):
        prows = slice((c // 2) * pair, (c // 2 + 1) * pair)
        keep = first_half if c % 2 == 0 else jnp.logical_not(first_half)
        for h in heads:
            kt = kt_ref[kcols[h], prows]
            kt = jnp.where(keep, kt, jnp.zeros_like(kt))
            kv = jnp.dot(kt, v_ref[prows, vcols[h]], preferred_element_type=F32)
            s = dec_ref[0, kcols[h], c:c + 1] * s_ref[h] + kv
            s_ref[h] = s
            sb_ref[c % 2, h] = s.astype(BF16)

    def emit_outputs(c):
        rows = slice(c * CHUNK, (c + 1) * CHUNK)
        outs = [jnp.dot(q_ref[rows, kcols[h]], sb_ref[c % 2, h], preferred_element_type=F32) for h in heads]
        for h in heads:
            o = outs[h]
            o = o * lax.rsqrt(jnp.mean(o * o, axis=-1, keepdims=True) + norm_eps) * ng[:, vcols[h]]
            o_ref[rows, vcols[h]] = (o * sr_ref[rows, vcols[h]].astype(F32)).astype(o_ref.dtype)

    advance_state(0)
    for c in range(n_chunks):
        if c + 1 < n_chunks:
            advance_state(c + 1)
        emit_outputs(c)


def _gla_scan(p, kt, dec_t, norm_g, batch, seq, kd, vd):
    m = p.shape[0]
    hk, hv = kd // GLA_HEADS, vd // GLA_HEADS
    ts = _tile(seq, 512)
    n_chunks = ts // CHUNK
    steps = seq // ts
    assert hk % LANES == 0 and hv % LANES == 0 and vd % kd == 0 and n_chunks % 2 == 0
    q_blk = 2 * vd // kd
    kern = functools.partial(_gla_scan_kernel, n_chunks=n_chunks, hk=hk, hv=hv)
    return pl.pallas_call(
        kern,
        grid=(batch, steps),
        in_specs=[
            pl.BlockSpec((ts, kd), lambda b, s: (b * steps + s, q_blk)),
            pl.BlockSpec((kd, ts), lambda b, s: (0, b * steps + s)),
            pl.BlockSpec((ts, vd), lambda b, s: (b * steps + s, 0)),
            pl.BlockSpec((ts, vd), lambda b, s: (b * steps + s, 1)),
            pl.BlockSpec((1, kd, n_chunks), lambda b, s: (b * steps + s, 0, 0)),
            pl.BlockSpec((1, vd), lambda b, s: (0, 0)),
        ],
        out_specs=pl.BlockSpec((ts, vd), lambda b, s: (b * steps + s, 0)),
        out_shape=jax.ShapeDtypeStruct((m, vd), BF16),
        scratch_shapes=[pltpu.VMEM((GLA_HEADS, hk, hv), F32), pltpu.VMEM((2, GLA_HEADS, hk, hv), BF16)],
        compiler_params=_params(("parallel", "arbitrary")),
        name="gla_scan",
    )(p, kt, p, p, dec_t, norm_g)


def _ffn_kernel(x_ref, g_ref, wu_ref, wd_ref, fg_ref, o_ref, hn_ref, *, final_norm):
    j = pl.program_id(1)
    last = pl.num_programs(1) - 1

    def mlp(hn):
        a = jnp.maximum(jnp.dot(hn, wu_ref[...], preferred_element_type=F32), 0.0)
        return jnp.dot((a * a).astype(BF16), wd_ref[...], preferred_element_type=F32)

    @pl.when(j == 0)
    def _():
        x = x_ref[...]
        hn = _rms_norm(x, g_ref[...]).astype(BF16)
        hn_ref[...] = hn
        o_ref[...] = x + mlp(hn)

    if final_norm:
        @pl.when((j > 0) & (j < last))
        def _():
            o_ref[...] += mlp(hn_ref[...])

        @pl.when(j == last)
        def _():
            o_ref[...] = _rms_norm(o_ref[...] + mlp(hn_ref[...]), fg_ref[...])
    else:
        @pl.when(j > 0)
        def _():
            o_ref[...] += mlp(hn_ref[...])


def _ffn(x, g, w_up, w_down, final_g, final_norm):
    m, d = x.shape
    f = w_up.shape[1]
    tm, tf = _tile(m, 1024), _tile(f, 512)
    assert f // tf >= 2
    kern = functools.partial(_ffn_kernel, final_norm=final_norm)
    return pl.pallas_call(
        kern,
        grid=(m // tm, f // tf),
        in_specs=[
            pl.BlockSpec((tm, d), lambda i, j: (i, 0)),
            pl.BlockSpec((1, d), lambda i, j: (0, 0)),
            pl.BlockSpec((d, tf), lambda i, j: (0, j)),
            pl.BlockSpec((tf, d), lambda i, j: (j, 0)),
            pl.BlockSpec((1, d), lambda i, j: (0, 0)),
        ],
        out_specs=pl.BlockSpec((tm, d), lambda i, j: (i, 0)),
        out_shape=jax.ShapeDtypeStruct((m, d), F32),
        scratch_shapes=[pltpu.VMEM((tm, d), BF16)],
        compiler_params=_params(("parallel", "arbitrary")),
        name="ffn",
    )(x, g, w_up, w_down, final_g)


def kernel(x, norm_mix_g, norm_ffn_g, final_g, gm_w_in, gm_ln_g, gm_ln_b, gm_w_s, gm_b_s, gm_w_out,
           gla_w_in, gla_w_a1, gla_w_a2, gla_b_a, gla_norm_g, gla_w_o, ffn_w_up, ffn_w_down):
    batch, seq, d = x.shape
    depth = norm_mix_g.shape[0]
    m = batch * seq
    h = x.reshape(m, d)
    final_g2 = final_g.reshape(1, d)

    for i in range(depth):
        jdx = i // 2
        mix_g = norm_mix_g[i].reshape(1, d)
        if i % 2 == 0:
            z = _gm_in(h, mix_g, gm_w_in[jdx].astype(BF16))
            gated = _gm_spatial(z, gm_ln_g[jdx].reshape(1, -1), gm_ln_b[jdx].reshape(1, -1),
                                gm_w_s[jdx], gm_b_s[jdx].T)
            h = _mm_residual(gated, gm_w_out[jdx].astype(BF16), h)
        else:
            kd = gla_w_a2.shape[2]
            vd = gla_w_o.shape[1]
            rank = gla_w_a1.shape[2]
            wa1 = jnp.pad(gla_w_a1[jdx], ((0, 0), (0, LANES - rank))).astype(BF16)
            wa2 = jnp.pad(gla_w_a2[jdx], ((0, LANES - rank), (0, 0))).astype(BF16)
            p, kt, dec = _gla_in(h, mix_g, gla_w_in[jdx].astype(BF16), wa1, wa2,
                                 gla_b_a[jdx].reshape(1, kd), vd)
            ts = _tile(seq, 512)
            dec_t = dec.reshape(m // ts, ts // CHUNK, kd).transpose(0, 2, 1)
            o = _gla_scan(p, kt, dec_t, gla_norm_g[jdx].reshape(1, vd), batch, seq, kd, vd)
            h = _mm_residual(o, gla_w_o[jdx].astype(BF16), h)
        h = _ffn(h, norm_ffn_g[i].reshape(1, d), ffn_w_up[i].astype(BF16), ffn_w_down[i].astype(BF16),
                 final_g2, final_norm=(i == depth - 1))
    return h.reshape(batch, seq, d)
```

```python
import functools

import jax
import jax.numpy as jnp
from jax import lax
from jax.experimental import pallas as pl
from jax.experimental.pallas import tpu as pltpu

CHUNK = 64
GM_BLOCK = 128
GM_GROUPS = 8
GLA_HEADS = 4
GLA_GATE_TAU = 16.0
EPS = 1e-6

LANES = 128
SUBLANES = 8
GM_OUT_SLAB_BLOCKS = 2
GLA_PROJ_SLABS = 4
GLA_GATE_LANES = 2 * LANES
V7X_VMEM_LIMIT_BYTES = 56 * 1024 * 1024

F32 = jnp.float32
BF16 = jnp.bfloat16


def _params(semantics):
    return pltpu.CompilerParams(dimension_semantics=semantics, vmem_limit_bytes=V7X_VMEM_LIMIT_BYTES)


def _tile(n, want):
    t = min(n, want)
    assert n % t == 0, (n, t)
    return t


def _rms_norm(x, g):
    return x * lax.rsqrt(jnp.mean(x * x, axis=-1, keepdims=True) + EPS) * g


def _gelu_exact(z):
    return 0.5 * z * (1.0 + lax.erf(z * 0.7071067811865476))


def _log_sigmoid(x):
    return jnp.minimum(x, 0.0) - jnp.log(1.0 + jnp.exp(-jnp.abs(x)))


def _gm_in_kernel(x_ref, g_ref, w_ref, z_ref, hn_ref):
    def project(hn):
        z = jnp.dot(hn, w_ref[...], preferred_element_type=F32)
        z_ref[...] = _gelu_exact(z).astype(z_ref.dtype)

    @pl.when(pl.program_id(1) == 0)
    def _():
        hn = _rms_norm(x_ref[...], g_ref[...]).astype(BF16)
        hn_ref[...] = hn
        project(hn)

    @pl.when(pl.program_id(1) != 0)
    def _():
        project(hn_ref[...])


def _gm_in(x, g, w):
    m, d = x.shape
    n = w.shape[1]
    tm, tn = _tile(m, 1024), _tile(n, 1024)
    return pl.pallas_call(
        _gm_in_kernel,
        grid=(m // tm, n // tn),
        in_specs=[
            pl.BlockSpec((tm, d), lambda i, j: (i, 0)),
            pl.BlockSpec((1, d), lambda i, j: (0, 0)),
            pl.BlockSpec((d, tn), lambda i, j: (0, j)),
        ],
        out_specs=pl.BlockSpec((tm, tn), lambda i, j: (i, j)),
        out_shape=jax.ShapeDtypeStruct((m, n), BF16),
        scratch_shapes=[pltpu.VMEM((tm, d), BF16)],
        compiler_params=_params(("parallel", "arbitrary")),
        name="gm_in",
    )(x, g, w)


def _gm_out_kernel(u_ref, v_ref, lng_ref, lnb_ref, ws_ref, bst_ref, w_ref, x_ref, o_ref, gated_ref, *,
                   n_blocks, group_dim):
    row_chunk = lax.broadcasted_iota(jnp.int32, (GM_BLOCK, GM_BLOCK), 0) // CHUNK
    col_chunk = lax.broadcasted_iota(jnp.int32, (GM_BLOCK, GM_BLOCK), 1) // CHUNK
    causal = col_chunk <= row_chunk
    mix_w = [jnp.where(causal, ws_ref[g], 0.0).astype(BF16) for g in range(GM_GROUPS)]
    bst = bst_ref[...]
    def gate_block(nb):
        rows = slice(nb * GM_BLOCK, (nb + 1) * GM_BLOCK)
        v = v_ref[rows, :].astype(F32)
        vc = v - jnp.mean(v, axis=-1, keepdims=True)
        vn = vc * lax.rsqrt(jnp.mean(vc * vc, axis=-1, keepdims=True) + EPS) * lng_ref[...] + lnb_ref[...]
        vn = vn.astype(BF16)
        for g in range(GM_GROUPS):
            cols = slice(g * group_dim, (g + 1) * group_dim)
            mixed = jnp.dot(mix_w[g], vn[:, cols], preferred_element_type=F32) + bst[:, g:g + 1]
            gated_ref[rows, cols] = (u_ref[rows, cols].astype(F32) * mixed).astype(gated_ref.dtype)

    for slab in range(n_blocks // GM_OUT_SLAB_BLOCKS):
        for nb in range(slab * GM_OUT_SLAB_BLOCKS, (slab + 1) * GM_OUT_SLAB_BLOCKS):
            gate_block(nb)
        rows = slice(slab * GM_OUT_SLAB_BLOCKS * GM_BLOCK, (slab + 1) * GM_OUT_SLAB_BLOCKS * GM_BLOCK)
        o_ref[rows, :] = x_ref[rows, :] + jnp.dot(gated_ref[rows, :], w_ref[...], preferred_element_type=F32)


def _gm_out(z, ln_g, ln_b, w_s, b_s_t, w_out, x):
    m, two_w = z.shape
    width = two_w // 2
    n = w_out.shape[1]
    tm = _tile(m, 512)
    assert tm % GM_BLOCK == 0 and width % (GM_GROUPS * LANES) == 0
    kern = functools.partial(_gm_out_kernel, n_blocks=tm // GM_BLOCK, group_dim=width // GM_GROUPS)
    return pl.pallas_call(
        kern,
        grid=(m // tm,),
        in_specs=[
            pl.BlockSpec((tm, width), lambda i: (i, 0)),
            pl.BlockSpec((tm, width), lambda i: (i, 1)),
            pl.BlockSpec((1, width), lambda i: (0, 0)),
            pl.BlockSpec((1, width), lambda i: (0, 0)),
            pl.BlockSpec((GM_GROUPS, GM_BLOCK, GM_BLOCK), lambda i: (0, 0, 0)),
            pl.BlockSpec((GM_BLOCK, GM_GROUPS), lambda i: (0, 0)),
            pl.BlockSpec((width, n), lambda i: (0, 0)),
            pl.BlockSpec((tm, n), lambda i: (i, 0)),
        ],
        out_specs=pl.BlockSpec((tm, n), lambda i: (i, 0)),
        out_shape=jax.ShapeDtypeStruct((m, n), F32),
        scratch_shapes=[pltpu.VMEM((tm, width), BF16)],
        compiler_params=_params(("parallel",)),
        name="gm_out",
    )(z, z, ln_g, ln_b, w_s, b_s_t, w_out, x)


def _mm_residual_kernel(a_ref, w_ref, x_ref, o_ref):
    o_ref[...] = x_ref[...] + jnp.dot(a_ref[...], w_ref[...], preferred_element_type=F32)


def _mm_residual(a, w, x):
    m, k = a.shape
    n = w.shape[1]
    tm = _tile(m, 512)
    return pl.pallas_call(
        _mm_residual_kernel,
        grid=(m // tm,),
        in_specs=[
            pl.BlockSpec((tm, k), lambda i: (i, 0)),
            pl.BlockSpec((k, n), lambda i: (0, 0)),
            pl.BlockSpec((tm, n), lambda i: (i, 0)),
        ],
        out_specs=pl.BlockSpec((tm, n), lambda i: (i, 0)),
        out_shape=jax.ShapeDtypeStruct((m, n), F32),
        compiler_params=_params(("parallel",)),
        name="mm_residual",
    )(a, w, x)


def _gla_w_tile(j, n_tiles):
    return jnp.where(j == 0, 0, jnp.where(j == n_tiles - 1, 1, j + 1))


def _gla_p_tile(j, n_tiles):
    return jnp.where(j == 0, n_tiles - 2, jnp.minimum(j - 1, n_tiles - 3))


def _gla_in_kernel(x_ref, g_ref, w_ref, wa1_ref, wa2_ref, ba_ref, p_ref, kt_ref, dec_ref, hn_ref, gs_ref, *,
                   n_vtiles, n_tiles):
    j = pl.program_id(1)
    last = n_tiles - 1

    def project(hn):
        return jnp.dot(hn, w_ref[...], preferred_element_type=F32)

    @pl.when(j == 0)
    def _():
        hn = _rms_norm(x_ref[...], g_ref[...]).astype(BF16)
        hn_ref[...] = hn
        p_ref[...] = project(hn).astype(p_ref.dtype)
        low = jnp.dot(hn, wa1_ref[...], preferred_element_type=F32)
        gs_ref[...] = jnp.dot(low.astype(BF16), wa2_ref[...], preferred_element_type=F32) + ba_ref[...]

    tm, kd = gs_ref.shape
    tn = p_ref.shape[1]
    cs = tn // GLA_PROJ_SLABS
    sub = SUBLANES

    def gate_piece(c, l):
        rows = slice(c * CHUNK, (c + 1) * CHUNK)
        cols = slice(l * GLA_GATE_LANES, (l + 1) * GLA_GATE_LANES)
        g = _log_sigmoid(gs_ref[rows, cols]) * (1.0 / GLA_GATE_TAU)
        g3 = g.reshape(CHUNK // sub, sub, GLA_GATE_LANES)
        pos = lax.broadcasted_iota(jnp.int32, g3.shape, 1)
        shift = 1
        while shift < sub:
            g3 = g3 + jnp.where(pos >= shift, pltpu.roll(g3, shift, axis=1), 0.0)
            shift *= 2
        groups = [g3[0]]
        for r in range(1, CHUNK // sub):
            groups.append(g3[r] + groups[r - 1][sub - 1:sub, :])
        g_end = groups[-1][sub - 1:sub, :]
        gs_ref[rows, cols] = jnp.concatenate([jnp.exp(g_end - gr) for gr in groups], axis=0)
        dec_ref[c:c + 1, cols] = jnp.exp(g_end)

    pieces = [(c, l) for c in range(tm // CHUNK) for l in range(kd // GLA_GATE_LANES)]
    n_gate_steps = n_tiles - 2
    per_slot = -(-len(pieces) // (n_gate_steps * GLA_PROJ_SLABS))

    def project_with_gate_pieces(step, epilogue):
        for s in range(GLA_PROJ_SLABS):
            slot = (step - 1) * GLA_PROJ_SLABS + s
            for c, l in pieces[slot * per_slot:(slot + 1) * per_slot]:
                gate_piece(c, l)
            cols = slice(s * cs, (s + 1) * cs)
            acc = jnp.dot(hn_ref[...], w_ref[:, cols], preferred_element_type=F32)
            p_ref[:, cols] = epilogue(acc).astype(p_ref.dtype)

    for step in range(1, n_tiles - 1):
        is_value = step <= n_vtiles
        pl.when(j == step)(functools.partial(
            project_with_gate_pieces, step,
            (lambda a: a) if is_value else (lambda r: r * jax.nn.sigmoid(r))))

    @pl.when(j == last)
    def _():
        kt_ref[...] = (project(hn_ref[...]) * gs_ref[...]).T.astype(kt_ref.dtype)


def _gla_in(x, g, w, wa1, wa2, ba, vd):
    m, d = x.shape
    n = w.shape[1]
    kd = wa2.shape[1]
    tm = _tile(m, 1024)
    tn = kd
    n_tiles = n // tn
    n_vtiles = vd // tn
    assert n == 2 * kd + 2 * vd and vd % tn == 0 and n_vtiles >= 2
    assert tm % CHUNK == 0 and (tm // CHUNK) % 8 == 0 and tm % LANES == 0
    rank_pad = wa1.shape[1]
    assert tn % (GLA_PROJ_SLABS * LANES) == 0 and kd % GLA_GATE_LANES == 0 and CHUNK % SUBLANES == 0
    kern = functools.partial(_gla_in_kernel, n_vtiles=n_vtiles, n_tiles=n_tiles)
    return pl.pallas_call(
        kern,
        grid=(m // tm, n_tiles),
        in_specs=[
            pl.BlockSpec((tm, d), lambda i, j: (i, 0)),
            pl.BlockSpec((1, d), lambda i, j: (0, 0)),
            pl.BlockSpec((d, tn), lambda i, j: (0, _gla_w_tile(j, n_tiles))),
            pl.BlockSpec((d, rank_pad), lambda i, j: (0, 0)),
            pl.BlockSpec((rank_pad, kd), lambda i, j: (0, 0)),
            pl.BlockSpec((1, kd), lambda i, j: (0, 0)),
        ],
        out_specs=[
            pl.BlockSpec((tm, tn), lambda i, j: (i, _gla_p_tile(j, n_tiles))),
            pl.BlockSpec((kd, tm), lambda i, j: (0, i)),
            pl.BlockSpec((tm // CHUNK, kd), lambda i, j: (i, 0)),
        ],
        out_shape=[
            jax.ShapeDtypeStruct((m, n - kd), BF16),
            jax.ShapeDtypeStruct((kd, m), BF16),
            jax.ShapeDtypeStruct((m // CHUNK, kd), F32),
        ],
        scratch_shapes=[pltpu.VMEM((tm, d), BF16), pltpu.VMEM((tm, kd), F32)],
        compiler_params=_params(("parallel", "arbitrary")),
        name="gla_in",
    )(x, g, w, wa1, wa2, ba)


def _gla_scan_kernel(q_ref, kt_ref, v_ref, sr_ref, dec_ref, ng_ref, o_ref, s_ref, sb_ref, *, n_chunks, hk, hv):
    @pl.when(pl.program_id(1) == 0)
    def _():
        s_ref[...] = jnp.zeros_like(s_ref)

    ng = ng_ref[...]
    norm_eps = EPS * hk
    pair = 2 * CHUNK
    first_half = lax.broadcasted_iota(jnp.int32, (hk, pair), 1) < CHUNK
    heads = range(GLA_HEADS)
    kcols = [slice(h * hk, (h + 1) * hk) for h in heads]
    vcols = [slice(h * hv, (h + 1) * hv) for h in heads]

    def advance_state(c):
        prows = slice((c // 2) * pair, (c // 2 + 1) * pair)
        keep = first_half if c % 2 == 0 else jnp.logical_not(first_half)
        for h in heads:
            kt = kt_ref[kcols[h], prows]
            kt = jnp.where(keep, kt, jnp.zeros_like(kt))
            kv = jnp.dot(kt, v_ref[prows, vcols[h]], preferred_element_type=F32)
            s = dec_ref[0, kcols[h], c:c + 1] * s_ref[h] + kv
            s_ref[h] = s
            sb_ref[c % 2, h] = s.astype(BF16)

    def emit_outputs(c):
        rows = slice(c * CHUNK, (c + 1) * CHUNK)
        outs = [jnp.dot(q_ref[rows, kcols[h]], sb_ref[c % 2, h], preferred_element_type=F32) for h in heads]
        for h in heads:
            o = outs[h]
            o = o * lax.rsqrt(jnp.mean(o * o, axis=-1, keepdims=True) + norm_eps) * ng[:, vcols[h]]
            o_ref[rows, vcols[h]] = (o * sr_ref[rows, vcols[h]].astype(F32)).astype(o_ref.dtype)

    advance_state(0)
    for c in range(n_chunks):
        if c + 1 < n_chunks:
            advance_state(c + 1)
        emit_outputs(c)


def _gla_scan(p, kt, dec_t, norm_g, batch, seq, kd, vd):
    m = p.shape[0]
    hk, hv = kd // GLA_HEADS, vd // GLA_HEADS
    ts = _tile(seq, 512)
    n_chunks = ts // CHUNK
    steps = seq // ts
    assert hk % LANES == 0 and hv % LANES == 0 and vd % kd == 0 and n_chunks % 2 == 0
    q_blk = 2 * vd // kd
    kern = functools.partial(_gla_scan_kernel, n_chunks=n_chunks, hk=hk, hv=hv)
    return pl.pallas_call(
        kern,
        grid=(batch, steps),
        in_specs=[
            pl.BlockSpec((ts, kd), lambda b, s: (b * steps + s, q_blk)),
            pl.BlockSpec((kd, ts), lambda b, s: (0, b * steps + s)),
            pl.BlockSpec((ts, vd), lambda b, s: (b * steps + s, 0)),
            pl.BlockSpec((ts, vd), lambda b, s: (b * steps + s, 1)),
            pl.BlockSpec((1, kd, n_chunks), lambda b, s: (b * steps + s, 0, 0)),
            pl.BlockSpec((1, vd), lambda b, s: (0, 0)),
        ],
        out_specs=pl.BlockSpec((ts, vd), lambda b, s: (b * steps + s, 0)),
        out_shape=jax.ShapeDtypeStruct((m, vd), BF16),
        scratch_shapes=[pltpu.VMEM((GLA_HEADS, hk, hv), F32), pltpu.VMEM((2, GLA_HEADS, hk, hv), BF16)],
        compiler_params=_params(("parallel", "arbitrary")),
        name="gla_scan",
    )(p, kt, p, p, dec_t, norm_g)


def _ffn_kernel(x_hbm, g_ref, wu_ref, wd_ref, fg_ref, o_ref, hn_ref, xs_ref, x_sem, *, final_norm):
    i, j = pl.program_id(0), pl.program_id(1)
    n_row_blocks = pl.num_programs(0)
    last = pl.num_programs(1) - 1
    tm = xs_ref.shape[0]

    def x_copy(row_block):
        rows = pl.ds(pl.multiple_of(row_block * tm, tm), tm)
        return pltpu.make_async_copy(x_hbm.at[rows, :], xs_ref, x_sem)

    def mlp(hn):
        a = jnp.maximum(jnp.dot(hn, wu_ref[...], preferred_element_type=F32), 0.0)
        return jnp.dot((a * a).astype(BF16), wd_ref[...], preferred_element_type=F32)

    @pl.when(j == 0)
    def _():
        @pl.when(i == 0)
        def _():
            x_copy(i).start()

        x_copy(i).wait()
        x = xs_ref[...]
        hn = _rms_norm(x, g_ref[...]).astype(BF16)
        hn_ref[...] = hn
        o_ref[...] = x + mlp(hn)

    @pl.when((j == 1) & (i + 1 < n_row_blocks))
    def _():
        x_copy(i + 1).start()

    if final_norm:
        @pl.when((j > 0) & (j < last))
        def _():
            o_ref[...] += mlp(hn_ref[...])

        @pl.when(j == last)
        def _():
            o_ref[...] = _rms_norm(o_ref[...] + mlp(hn_ref[...]), fg_ref[...])
    else:
        @pl.when(j > 0)
        def _():
            o_ref[...] += mlp(hn_ref[...])


def _ffn(x, g, w_up, w_down, final_g, final_norm):
    m, d = x.shape
    f = w_up.shape[1]
    tm, tf = _tile(m, 1024), _tile(f, 1024)
    assert f // tf >= 3
    kern = functools.partial(_ffn_kernel, final_norm=final_norm)
    return pl.pallas_call(
        kern,
        grid=(m // tm, f // tf),
        in_specs=[
            pl.BlockSpec(memory_space=pl.ANY),
            pl.BlockSpec((1, d), lambda i, j: (0, 0)),
            pl.BlockSpec((d, tf), lambda i, j: (0, j)),
            pl.BlockSpec((tf, d), lambda i, j: (j, 0)),
            pl.BlockSpec((1, d), lambda i, j: (0, 0)),
        ],
        out_specs=pl.BlockSpec((tm, d), lambda i, j: (i, 0)),
        out_shape=jax.ShapeDtypeStruct((m, d), F32),
        scratch_shapes=[pltpu.VMEM((tm, d), BF16), pltpu.VMEM((tm, d), F32), pltpu.SemaphoreType.DMA(())],
        compiler_params=_params(("arbitrary", "arbitrary")),
        name="ffn",
    )(x, g, w_up, w_down, final_g)


def kernel(x, norm_mix_g, norm_ffn_g, final_g, gm_w_in, gm_ln_g, gm_ln_b, gm_w_s, gm_b_s, gm_w_out,
           gla_w_in, gla_w_a1, gla_w_a2, gla_b_a, gla_norm_g, gla_w_o, ffn_w_up, ffn_w_down):
    batch, seq, d = x.shape
    depth = norm_mix_g.shape[0]
    m = batch * seq
    h = x.reshape(m, d)
    final_g2 = final_g.reshape(1, d)

    for i in range(depth):
        jdx = i // 2
        mix_g = norm_mix_g[i].reshape(1, d)
        if i % 2 == 0:
            z = _gm_in(h, mix_g, gm_w_in[jdx].astype(BF16))
            h = _gm_out(z, gm_ln_g[jdx].reshape(1, -1), gm_ln_b[jdx].reshape(1, -1),
                        gm_w_s[jdx], gm_b_s[jdx].T, gm_w_out[jdx].astype(BF16), h)
        else:
            kd = gla_w_a2.shape[2]
            vd = gla_w_o.shape[1]
            rank = gla_w_a1.shape[2]
            wa1 = jnp.pad(gla_w_a1[jdx], ((0, 0), (0, LANES - rank))).astype(BF16)
            wa2 = jnp.pad(gla_w_a2[jdx], ((0, LANES - rank), (0, 0))).astype(BF16)
            p, kt, dec = _gla_in(h, mix_g, gla_w_in[jdx].astype(BF16), wa1, wa2,
                                 gla_b_a[jdx].reshape(1, kd), vd)
            ts = _tile(seq, 512)
            dec_t = dec.reshape(m // ts, ts // CHUNK, kd).transpose(0, 2, 1)
            o = _gla_scan(p, kt, dec_t, gla_norm_g[jdx].reshape(1, vd), batch, seq, kd, vd)
            h = _mm_residual(o, gla_w_o[jdx].astype(BF16), h)
        h = _ffn(h, norm_ffn_g[i].reshape(1, d), ffn_w_up[i].astype(BF16), ffn_w_down[i].astype(BF16),
                 final_g2, final_norm=(i == depth - 1))
    return h.reshape(batch, seq, d)
```

```python
import functools

import jax
import jax.numpy as jnp
from jax import lax
from jax.experimental import pallas as pl
from jax.experimental.pallas import tpu as pltpu

CHUNK = 64
GM_BLOCK = 128
GM_GROUPS = 8
GLA_HEADS = 4
GLA_GATE_TAU = 16.0
EPS = 1e-6

LANES = 128
SUBLANES = 8
BF16_SUBLANES = 16
GM_OUT_SLAB_BLOCKS = 2
GLA_PROJ_SLABS = 4
GLA_GATE_LANES = 2 * LANES
V7X_VMEM_LIMIT_BYTES = 56 * 1024 * 1024

F32 = jnp.float32
BF16 = jnp.bfloat16


def _params(semantics):
    return pltpu.CompilerParams(dimension_semantics=semantics, vmem_limit_bytes=V7X_VMEM_LIMIT_BYTES)


def _tile(n, want):
    t = min(n, want)
    assert n % t == 0, (n, t)
    return t


def _rms_norm(x, g):
    return x * lax.rsqrt(jnp.mean(x * x, axis=-1, keepdims=True) + EPS) * g


def _gelu_exact(z):
    return 0.5 * z * (1.0 + lax.erf(z * 0.7071067811865476))


def _log_sigmoid(x):
    return jnp.minimum(x, 0.0) - jnp.log(1.0 + jnp.exp(-jnp.abs(x)))


def _ffn_weight_cast_specs(w_up, w_down, layer, n_slices, slice_index):
    _, d, f = w_up.shape
    ru, rd = d // n_slices, f // n_slices
    assert d % n_slices == 0 and f % n_slices == 0 and ru % BF16_SUBLANES == 0
    in_specs = [pl.BlockSpec((None, ru, f), lambda *g: (layer, slice_index(*g), 0)),
                pl.BlockSpec((None, rd, d), lambda *g: (layer, slice_index(*g), 0))]
    out_specs = [pl.BlockSpec((ru, f), lambda *g: (slice_index(*g), 0)),
                 pl.BlockSpec((rd, d), lambda *g: (slice_index(*g), 0))]
    out_shape = [jax.ShapeDtypeStruct((d, f), BF16), jax.ShapeDtypeStruct((f, d), BF16)]
    return in_specs, out_specs, out_shape


def _cast_ffn_weight_slices(wu32_ref, wd32_ref, wu16_ref, wd16_ref):
    wu16_ref[...] = wu32_ref[...].astype(wu16_ref.dtype)
    wd16_ref[...] = wd32_ref[...].astype(wd16_ref.dtype)


def _gm_in_kernel(x_ref, g_ref, w_ref, wu32_ref, wd32_ref, z_ref, wu16_ref, wd16_ref, hn_ref):
    _cast_ffn_weight_slices(wu32_ref, wd32_ref, wu16_ref, wd16_ref)

    def project(hn):
        z = jnp.dot(hn, w_ref[...], preferred_element_type=F32)
        z_ref[...] = _gelu_exact(z).astype(z_ref.dtype)

    @pl.when(pl.program_id(1) == 0)
    def _():
        hn = _rms_norm(x_ref[...], g_ref[...]).astype(BF16)
        hn_ref[...] = hn
        project(hn)

    @pl.when(pl.program_id(1) != 0)
    def _():
        project(hn_ref[...])


def _gm_in(x, g, w, layer, ffn_w_up, ffn_w_down, ffn_layer):
    m, d = x.shape
    n = w.shape[2]
    tm, tn = _tile(m, 1024), _tile(n, 2048)
    n_col = n // tn
    cast_in, cast_out, cast_shape = _ffn_weight_cast_specs(
        ffn_w_up, ffn_w_down, ffn_layer, (m // tm) * n_col, lambda i, j: i * n_col + j)
    return pl.pallas_call(
        _gm_in_kernel,
        grid=(m // tm, n_col),
        in_specs=[
            pl.BlockSpec((tm, d), lambda i, j: (i, 0)),
            pl.BlockSpec((1, d), lambda i, j: (0, 0)),
            pl.BlockSpec((None, d, tn), lambda i, j: (layer, 0, j)),
        ] + cast_in,
        out_specs=[pl.BlockSpec((tm, tn), lambda i, j: (i, j))] + cast_out,
        out_shape=[jax.ShapeDtypeStruct((m, n), BF16)] + cast_shape,
        scratch_shapes=[pltpu.VMEM((tm, d), BF16)],
        compiler_params=_params(("parallel", "arbitrary")),
        name="gm_in",
    )(x, g, w, ffn_w_up, ffn_w_down)


def _gm_out_kernel(u_ref, v_ref, lng_ref, lnb_ref, ws_ref, bst_ref, w_ref, x_ref, o_ref, gated_ref, *,
                   n_blocks, group_dim):
    row_chunk = lax.broadcasted_iota(jnp.int32, (GM_BLOCK, GM_BLOCK), 0) // CHUNK
    col_chunk = lax.broadcasted_iota(jnp.int32, (GM_BLOCK, GM_BLOCK), 1) // CHUNK
    causal = col_chunk <= row_chunk
    mix_w = [jnp.where(causal, ws_ref[g], 0.0).astype(BF16) for g in range(GM_GROUPS)]
    bst = bst_ref[...]
    def gate_block(nb):
        rows = slice(nb * GM_BLOCK, (nb + 1) * GM_BLOCK)
        v = v_ref[rows, :].astype(F32)
        vc = v - jnp.mean(v, axis=-1, keepdims=True)
        vn = vc * lax.rsqrt(jnp.mean(vc * vc, axis=-1, keepdims=True) + EPS) * lng_ref[...] + lnb_ref[...]
        vn = vn.astype(BF16)
        for g in range(GM_GROUPS):
            cols = slice(g * group_dim, (g + 1) * group_dim)
            mixed = jnp.dot(mix_w[g], vn[:, cols], preferred_element_type=F32) + bst[:, g:g + 1]
            gated_ref[rows, cols] = (u_ref[rows, cols].astype(F32) * mixed).astype(gated_ref.dtype)

    for slab in range(n_blocks // GM_OUT_SLAB_BLOCKS):
        for nb in range(slab * GM_OUT_SLAB_BLOCKS, (slab + 1) * GM_OUT_SLAB_BLOCKS):
            gate_block(nb)
        rows = slice(slab * GM_OUT_SLAB_BLOCKS * GM_BLOCK, (slab + 1) * GM_OUT_SLAB_BLOCKS * GM_BLOCK)
        o_ref[rows, :] = x_ref[rows, :] + jnp.dot(gated_ref[rows, :], w_ref[...], preferred_element_type=F32)


def _gm_out(z, ln_g, ln_b, w_s, b_s_t, w_out, layer, x):
    m, two_w = z.shape
    width = two_w // 2
    n = w_out.shape[2]
    tm = _tile(m, 512)
    assert tm % GM_BLOCK == 0 and width % (GM_GROUPS * LANES) == 0
    kern = functools.partial(_gm_out_kernel, n_blocks=tm // GM_BLOCK, group_dim=width // GM_GROUPS)
    return pl.pallas_call(
        kern,
        grid=(m // tm,),
        in_specs=[
            pl.BlockSpec((tm, width), lambda i: (i, 0)),
            pl.BlockSpec((tm, width), lambda i: (i, 1)),
            pl.BlockSpec((1, width), lambda i: (0, 0)),
            pl.BlockSpec((1, width), lambda i: (0, 0)),
            pl.BlockSpec((GM_GROUPS, GM_BLOCK, GM_BLOCK), lambda i: (0, 0, 0)),
            pl.BlockSpec((GM_BLOCK, GM_GROUPS), lambda i: (0, 0)),
            pl.BlockSpec((None, width, n), lambda i: (layer, 0, 0)),
            pl.BlockSpec((tm, n), lambda i: (i, 0)),
        ],
        out_specs=pl.BlockSpec((tm, n), lambda i: (i, 0)),
        out_shape=jax.ShapeDtypeStruct((m, n), F32),
        scratch_shapes=[pltpu.VMEM((tm, width), BF16)],
        compiler_params=_params(("parallel",)),
        name="gm_out",
    )(z, z, ln_g, ln_b, w_s, b_s_t, w_out, x)


def _mm_residual_kernel(a_ref, w_ref, x_ref, o_ref):
    o_ref[...] = x_ref[...] + jnp.dot(a_ref[...], w_ref[...], preferred_element_type=F32)


def _mm_residual(a, w, layer, x):
    m, k = a.shape
    n = w.shape[2]
    tm = _tile(m, 512)
    return pl.pallas_call(
        _mm_residual_kernel,
        grid=(m // tm,),
        in_specs=[
            pl.BlockSpec((tm, k), lambda i: (i, 0)),
            pl.BlockSpec((None, k, n), lambda i: (layer, 0, 0)),
            pl.BlockSpec((tm, n), lambda i: (i, 0)),
        ],
        out_specs=pl.BlockSpec((tm, n), lambda i: (i, 0)),
        out_shape=jax.ShapeDtypeStruct((m, n), F32),
        compiler_params=_params(("parallel",)),
        name="mm_residual",
    )(a, w, x)


def _gla_w_tile(j, n_tiles):
    return jnp.where(j == 0, 0, jnp.where(j == n_tiles - 1, 1, j + 1))


def _gla_p_tile(j, n_tiles):
    return jnp.where(j == 0, n_tiles - 2, jnp.minimum(j - 1, n_tiles - 3))


def _gla_in_kernel(x_ref, g_ref, w_ref, wa1_ref, wa2_ref, ba_ref, wu32_ref, wd32_ref,
                   p_ref, kt_ref, dec_ref, wu16_ref, wd16_ref, hn_ref, gs_ref, *, n_vtiles, n_tiles):
    j = pl.program_id(1)
    last = n_tiles - 1
    _cast_ffn_weight_slices(wu32_ref, wd32_ref, wu16_ref, wd16_ref)

    def project(hn):
        return jnp.dot(hn, w_ref[...], preferred_element_type=F32)

    @pl.when(j == 0)
    def _():
        hn = _rms_norm(x_ref[...], g_ref[...]).astype(BF16)
        hn_ref[...] = hn
        p_ref[...] = project(hn).astype(p_ref.dtype)
        low = jnp.dot(hn, wa1_ref[...], preferred_element_type=F32)
        gs_ref[...] = jnp.dot(low.astype(BF16), wa2_ref[...], preferred_element_type=F32) + ba_ref[...]

    tm, kd = gs_ref.shape
    tn = p_ref.shape[1]
    cs = tn // GLA_PROJ_SLABS
    sub = SUBLANES

    def gate_piece(c, l):
        rows = slice(c * CHUNK, (c + 1) * CHUNK)
        cols = slice(l * GLA_GATE_LANES, (l + 1) * GLA_GATE_LANES)
        g = _log_sigmoid(gs_ref[rows, cols]) * (1.0 / GLA_GATE_TAU)
        g3 = g.reshape(CHUNK // sub, sub, GLA_GATE_LANES)
        pos = lax.broadcasted_iota(jnp.int32, g3.shape, 1)
        shift = 1
        while shift < sub:
            g3 = g3 + jnp.where(pos >= shift, pltpu.roll(g3, shift, axis=1), 0.0)
            shift *= 2
        groups = [g3[0]]
        for r in range(1, CHUNK // sub):
            groups.append(g3[r] + groups[r - 1][sub - 1:sub, :])
        g_end = groups[-1][sub - 1:sub, :]
        gs_ref[rows, cols] = jnp.concatenate([jnp.exp(g_end - gr) for gr in groups], axis=0)
        dec_ref[c:c + 1, cols] = jnp.exp(g_end)

    pieces = [(c, l) for c in range(tm // CHUNK) for l in range(kd // GLA_GATE_LANES)]
    n_gate_steps = n_tiles - 2
    per_slot = -(-len(pieces) // (n_gate_steps * GLA_PROJ_SLABS))

    def project_with_gate_pieces(step, epilogue):
        for s in range(GLA_PROJ_SLABS):
            slot = (step - 1) * GLA_PROJ_SLABS + s
            for c, l in pieces[slot * per_slot:(slot + 1) * per_slot]:
                gate_piece(c, l)
            cols = slice(s * cs, (s + 1) * cs)
            acc = jnp.dot(hn_ref[...], w_ref[:, cols], preferred_element_type=F32)
            p_ref[:, cols] = epilogue(acc).astype(p_ref.dtype)

    for step in range(1, n_tiles - 1):
        is_value = step <= n_vtiles
        pl.when(j == step)(functools.partial(
            project_with_gate_pieces, step,
            (lambda a: a) if is_value else (lambda r: r * jax.nn.sigmoid(r))))

    @pl.when(j == last)
    def _():
        kt_ref[...] = (project(hn_ref[...]) * gs_ref[...]).T.astype(kt_ref.dtype)


def _gla_in(x, g, w, layer, wa1, wa2, ba, vd, ffn_w_up, ffn_w_down, ffn_layer):
    m, d = x.shape
    n = w.shape[2]
    kd = wa2.shape[1]
    tm = _tile(m, 1024)
    tn = kd
    n_tiles = n // tn
    n_vtiles = vd // tn
    assert n == 2 * kd + 2 * vd and vd % tn == 0 and n_vtiles >= 2
    assert tm % CHUNK == 0 and (tm // CHUNK) % 8 == 0 and tm % LANES == 0
    rank_pad = wa1.shape[1]
    assert tn % (GLA_PROJ_SLABS * LANES) == 0 and kd % GLA_GATE_LANES == 0 and CHUNK % SUBLANES == 0
    kern = functools.partial(_gla_in_kernel, n_vtiles=n_vtiles, n_tiles=n_tiles)
    cast_steps = n_tiles - 2
    cast_in, cast_out, cast_shape = _ffn_weight_cast_specs(
        ffn_w_up, ffn_w_down, ffn_layer, (m // tm) * cast_steps,
        lambda i, j: i * cast_steps + jnp.minimum(j, cast_steps - 1))
    return pl.pallas_call(
        kern,
        grid=(m // tm, n_tiles),
        in_specs=[
            pl.BlockSpec((tm, d), lambda i, j: (i, 0)),
            pl.BlockSpec((1, d), lambda i, j: (0, 0)),
            pl.BlockSpec((None, d, tn), lambda i, j: (layer, 0, _gla_w_tile(j, n_tiles))),
            pl.BlockSpec((d, rank_pad), lambda i, j: (0, 0)),
            pl.BlockSpec((rank_pad, kd), lambda i, j: (0, 0)),
            pl.BlockSpec((1, kd), lambda i, j: (0, 0)),
        ] + cast_in,
        out_specs=[
            pl.BlockSpec((tm, tn), lambda i, j: (i, _gla_p_tile(j, n_tiles))),
            pl.BlockSpec((kd, tm), lambda i, j: (0, i)),
            pl.BlockSpec((tm // CHUNK, kd), lambda i, j: (i, 0)),
        ] + cast_out,
        out_shape=[
            jax.ShapeDtypeStruct((m, n - kd), BF16),
            jax.ShapeDtypeStruct((kd, m), BF16),
            jax.ShapeDtypeStruct((m // CHUNK, kd), F32),
        ] + cast_shape,
        scratch_shapes=[pltpu.VMEM((tm, d), BF16), pltpu.VMEM((tm, kd), F32)],
        compiler_params=_params(("parallel", "arbitrary")),
        name="gla_in",
    )(x, g, w, wa1, wa2, ba, ffn_w_up, ffn_w_down)


def _gla_scan_kernel(q_ref, kt_ref, v_ref, sr_ref, dec_ref, ng_ref, o_ref, s_ref, sb_ref, *, n_chunks, hk, hv):
    @pl.when(pl.program_id(1) == 0)
    def _():
        s_ref[...] = jnp.zeros_like(s_ref)

    ng = ng_ref[...]
    norm_eps = EPS * hk
    pair = 2 * CHUNK
    first_half = lax.broadcasted_iota(jnp.int32, (hk, pair), 1) < CHUNK
    heads = range(GLA_HEADS)
    kcols = [slice(h * hk, (h + 1) * hk) for h in heads]
    vcols = [slice(h * hv, (h + 1) * hv) for h in heads]

    def advance_state(c):
        prows = slice((c // 2) * pair, (c // 2 + 1) * pair)
        keep = first_half if c % 2 == 0 else jnp.logical_not(first_half)
        for h in heads:
            kt = kt_ref[kcols[h], prows]
            kt = jnp.where(keep, kt, jnp.zeros_like(kt))
            kv = jnp.dot(kt, v_ref[prows, vcols[h]], preferred_element_type=F32)
            if c == 0:
                s = dec_ref[0, kcols[h], 0:1] * s_ref[h] + kv
            else:
                s = s_ref[h] + kv
            sb_ref[c % 2, h] = s.astype(BF16)
            s_ref[h] = s * dec_ref[0, kcols[h], c + 1:c + 2] if c + 1 < n_chunks else s

    def emit_outputs(c):
        rows = slice(c * CHUNK, (c + 1) * CHUNK)
        outs = [jnp.dot(q_ref[rows, kcols[h]], sb_ref[c % 2, h], preferred_element_type=F32) for h in heads]
        for h in heads:
            o = outs[h]
            o = o * lax.rsqrt(jnp.mean(o * o, axis=-1, keepdims=True) + norm_eps) * ng[:, vcols[h]]
            o_ref[rows, vcols[h]] = (o * sr_ref[rows, vcols[h]].astype(F32)).astype(o_ref.dtype)

    advance_state(0)
    for c in range(n_chunks):
        if c + 1 < n_chunks:
            advance_state(c + 1)
        emit_outputs(c)


def _gla_scan(p, kt, dec_t, norm_g, batch, seq, kd, vd):
    m = p.shape[0]
    hk, hv = kd // GLA_HEADS, vd // GLA_HEADS
    ts = _tile(seq, 512)
    n_chunks = ts // CHUNK
    steps = seq // ts
    assert hk % LANES == 0 and hv % LANES == 0 and vd % kd == 0 and n_chunks % 2 == 0
    q_blk = 2 * vd // kd
    kern = functools.partial(_gla_scan_kernel, n_chunks=n_chunks, hk=hk, hv=hv)
    return pl.pallas_call(
        kern,
        grid=(batch, steps),
        in_specs=[
            pl.BlockSpec((ts, kd), lambda b, s: (b * steps + s, q_blk)),
            pl.BlockSpec((kd, ts), lambda b, s: (0, b * steps + s)),
            pl.BlockSpec((ts, vd), lambda b, s: (b * steps + s, 0)),
            pl.BlockSpec((ts, vd), lambda b, s: (b * steps + s, 1)),
            pl.BlockSpec((1, kd, n_chunks), lambda b, s: (b * steps + s, 0, 0)),
            pl.BlockSpec((1, vd), lambda b, s: (0, 0)),
        ],
        out_specs=pl.BlockSpec((ts, vd), lambda b, s: (b * steps + s, 0)),
        out_shape=jax.ShapeDtypeStruct((m, vd), BF16),
        scratch_shapes=[pltpu.VMEM((GLA_HEADS, hk, hv), F32), pltpu.VMEM((2, GLA_HEADS, hk, hv), BF16)],
        compiler_params=_params(("parallel", "arbitrary")),
        name="gla_scan",
    )(p, kt, p, p, dec_t, norm_g)


def _ffn_kernel(x_hbm, g_ref, wu_ref, wd_ref, fg_ref, o_ref, hn_ref, xs_ref, x_sem, *, final_norm):
    i, j = pl.program_id(0), pl.program_id(1)
    n_row_blocks = pl.num_programs(0)
    last = pl.num_programs(1) - 1
    tm = xs_ref.shape[0]

    def x_copy(row_block):
        rows = pl.ds(pl.multiple_of(row_block * tm, tm), tm)
        return pltpu.make_async_copy(x_hbm.at[rows, :], xs_ref, x_sem)

    def mlp(hn):
        a = jnp.maximum(jnp.dot(hn, wu_ref[...], preferred_element_type=F32), 0.0)
        return jnp.dot((a * a).astype(BF16), wd_ref[...], preferred_element_type=F32)

    @pl.when(j == 0)
    def _():
        @pl.when(i == 0)
        def _():
            x_copy(i).start()

        x_copy(i).wait()
        x = xs_ref[...]
        hn = _rms_norm(x, g_ref[...]).astype(BF16)
        hn_ref[...] = hn
        o_ref[...] = x + mlp(hn)

    @pl.when((j == 1) & (i + 1 < n_row_blocks))
    def _():
        x_copy(i + 1).start()

    if final_norm:
        @pl.when((j > 0) & (j < last))
        def _():
            o_ref[...] += mlp(hn_ref[...])

        @pl.when(j == last)
        def _():
            o_ref[...] = _rms_norm(o_ref[...] + mlp(hn_ref[...]), fg_ref[...])
    else:
        @pl.when(j > 0)
        def _():
            o_ref[...] += mlp(hn_ref[...])


def _ffn(x, g, w_up, w_down, final_g, final_norm):
    m, d = x.shape
    f = w_up.shape[1]
    tm, tf = _tile(m, 1024), _tile(f, 1024)
    assert f // tf >= 3
    kern = functools.partial(_ffn_kernel, final_norm=final_norm)
    return pl.pallas_call(
        kern,
        grid=(m // tm, f // tf),
        in_specs=[
            pl.BlockSpec(memory_space=pl.ANY),
            pl.BlockSpec((1, d), lambda i, j: (0, 0)),
            pl.BlockSpec((d, tf), lambda i, j: (0, j)),
            pl.BlockSpec((tf, d), lambda i, j: (j, 0)),
            pl.BlockSpec((1, d), lambda i, j: (0, 0)),
        ],
        out_specs=pl.BlockSpec((tm, d), lambda i, j: (i, 0)),
        out_shape=jax.ShapeDtypeStruct((m, d), F32),
        scratch_shapes=[pltpu.VMEM((tm, d), BF16), pltpu.VMEM((tm, d), F32), pltpu.SemaphoreType.DMA(())],
        compiler_params=_params(("arbitrary", "arbitrary")),
        name="ffn",
    )(x, g, w_up, w_down, final_g)


def kernel(x, norm_mix_g, norm_ffn_g, final_g, gm_w_in, gm_ln_g, gm_ln_b, gm_w_s, gm_b_s, gm_w_out,
           gla_w_in, gla_w_a1, gla_w_a2, gla_b_a, gla_norm_g, gla_w_o, ffn_w_up, ffn_w_down):
    batch, seq, d = x.shape
    depth = norm_mix_g.shape[0]
    m = batch * seq
    h = x.reshape(m, d)
    final_g2 = final_g.reshape(1, d)

    gm_w_in, gm_w_out, gla_w_in, gla_w_o = (w.astype(BF16) for w in (gm_w_in, gm_w_out, gla_w_in, gla_w_o))

    for i in range(depth):
        jdx = i // 2
        mix_g = norm_mix_g[i].reshape(1, d)
        if i % 2 == 0:
            z, w_up, w_down = _gm_in(h, mix_g, gm_w_in, jdx, ffn_w_up, ffn_w_down, i)
            h = _gm_out(z, gm_ln_g[jdx].reshape(1, -1), gm_ln_b[jdx].reshape(1, -1),
                        gm_w_s[jdx], gm_b_s[jdx].T, gm_w_out, jdx, h)
        else:
            kd = gla_w_a2.shape[2]
            vd = gla_w_o.shape[1]
            rank = gla_w_a1.shape[2]
            wa1 = jnp.pad(gla_w_a1[jdx], ((0, 0), (0, LANES - rank))).astype(BF16)
            wa2 = jnp.pad(gla_w_a2[jdx], ((0, LANES - rank), (0, 0))).astype(BF16)
            p, kt, dec, w_up, w_down = _gla_in(h, mix_g, gla_w_in, jdx, wa1, wa2, gla_b_a[jdx].reshape(1, kd), vd,
                                               ffn_w_up, ffn_w_down, i)
            ts = _tile(seq, 512)
            dec_t = dec.reshape(m // ts, ts // CHUNK, kd).transpose(0, 2, 1)
            o = _gla_scan(p, kt, dec_t, gla_norm_g[jdx].reshape(1, vd), batch, seq, kd, vd)
            h = _mm_residual(o, gla_w_o, jdx, h)
        h = _ffn(h, norm_ffn_g[i].reshape(1, d), w_up, w_down, final_g2, final_norm=(i == depth - 1))
    return h.reshape(batch, seq, d)
```

```python
import functools

import jax
import jax.numpy as jnp
from jax import lax
from jax.experimental import pallas as pl
from jax.experimental.pallas import tpu as pltpu

CHUNK = 64
GM_BLOCK = 128
GM_GROUPS = 8
GLA_HEADS = 4
GLA_GATE_TAU = 16.0
EPS = 1e-6

LANES = 128
SUBLANES = 8
BF16_SUBLANES = 16
FFN_COLS = 1024
GM_MIX_COLS = 2048
GM_OUT_SLAB_BLOCKS = 2
GLA_PROJ_SLABS = 4
GLA_GATE_LANES = 2 * LANES
V7X_VMEM_LIMIT_BYTES = 56 * 1024 * 1024

F32 = jnp.float32
BF16 = jnp.bfloat16


def _params(semantics):
    return pltpu.CompilerParams(dimension_semantics=semantics, vmem_limit_bytes=V7X_VMEM_LIMIT_BYTES)


def _tile(n, want):
    t = min(n, want)
    assert n % t == 0, (n, t)
    return t


def _rms_norm(x, g):
    return x * lax.rsqrt(jnp.mean(x * x, axis=-1, keepdims=True) + EPS) * g


def _gelu_exact(z):
    return 0.5 * z * (1.0 + lax.erf(z * 0.7071067811865476))


def _log_sigmoid(x):
    return jnp.minimum(x, 0.0) - jnp.log(1.0 + jnp.exp(-jnp.abs(x)))


def _ffn_weight_cast_specs(w_up, w_down, layer, n_slices, slice_index):
    _, d, f = w_up.shape
    ru, rd = d // n_slices, f // n_slices
    tf = _tile(f, FFN_COLS)
    assert d % n_slices == 0 and f % n_slices == 0 and ru % BF16_SUBLANES == 0
    in_specs = [pl.BlockSpec((None, ru, f), lambda *g: (layer, slice_index(*g), 0)),
                pl.BlockSpec((None, rd, d), lambda *g: (layer, slice_index(*g), 0))]
    out_specs = [pl.BlockSpec((f // tf, ru, tf), lambda *g: (0, slice_index(*g), 0)),
                 pl.BlockSpec((rd, d), lambda *g: (slice_index(*g), 0))]
    out_shape = [jax.ShapeDtypeStruct((f // tf, d, tf), BF16), jax.ShapeDtypeStruct((f, d), BF16)]
    return in_specs, out_specs, out_shape


def _cast_ffn_weight_slices(wu32_ref, wd32_ref, wu16_ref, wd16_ref):
    n_tiles, _, tf = wu16_ref.shape
    for t in range(n_tiles):
        wu16_ref[t] = wu32_ref[:, t * tf:(t + 1) * tf].astype(wu16_ref.dtype)
    wd16_ref[...] = wd32_ref[...].astype(wd16_ref.dtype)


def _gm_in_kernel(x_ref, g_ref, w_ref, wu32_ref, wd32_ref, z_ref, wu16_ref, wd16_ref, hn_ref):
    _cast_ffn_weight_slices(wu32_ref, wd32_ref, wu16_ref, wd16_ref)

    def project(hn):
        z = jnp.dot(hn, w_ref[...], preferred_element_type=F32)
        z_ref[...] = _gelu_exact(z).astype(z_ref.dtype)

    @pl.when(pl.program_id(1) == 0)
    def _():
        hn = _rms_norm(x_ref[...], g_ref[...]).astype(BF16)
        hn_ref[...] = hn
        project(hn)

    @pl.when(pl.program_id(1) != 0)
    def _():
        project(hn_ref[...])


def _gm_in(x, g, w, layer, ffn_w_up, ffn_w_down, ffn_layer):
    m, d = x.shape
    _, n_col, _, tn = w.shape
    n = n_col * tn
    tm = _tile(m, 1024)
    cast_in, cast_out, cast_shape = _ffn_weight_cast_specs(
        ffn_w_up, ffn_w_down, ffn_layer, (m // tm) * n_col, lambda i, j: i * n_col + j)
    return pl.pallas_call(
        _gm_in_kernel,
        grid=(m // tm, n_col),
        in_specs=[
            pl.BlockSpec((tm, d), lambda i, j: (i, 0)),
            pl.BlockSpec((1, d), lambda i, j: (0, 0)),
            pl.BlockSpec((None, None, d, tn), lambda i, j: (layer, j, 0, 0)),
        ] + cast_in,
        out_specs=[pl.BlockSpec((tm, tn), lambda i, j: (i, j))] + cast_out,
        out_shape=[jax.ShapeDtypeStruct((m, n), BF16)] + cast_shape,
        scratch_shapes=[pltpu.VMEM((tm, d), BF16)],
        compiler_params=_params(("parallel", "arbitrary")),
        name="gm_in",
    )(x, g, w, ffn_w_up, ffn_w_down)


def _gm_out_kernel(u_ref, v_ref, lng_ref, lnb_ref, ws_ref, bst_ref, w_ref, x_ref, o_ref, gated_ref, *,
                   n_blocks, group_dim):
    row_chunk = lax.broadcasted_iota(jnp.int32, (GM_BLOCK, GM_BLOCK), 0) // CHUNK
    col_chunk = lax.broadcasted_iota(jnp.int32, (GM_BLOCK, GM_BLOCK), 1) // CHUNK
    causal = col_chunk <= row_chunk
    mix_w = [jnp.where(causal, ws_ref[g], 0.0).astype(BF16) for g in range(GM_GROUPS)]
    bst = bst_ref[...]
    def gate_block(nb):
        rows = slice(nb * GM_BLOCK, (nb + 1) * GM_BLOCK)
        v = v_ref[rows, :].astype(F32)
        vc = v - jnp.mean(v, axis=-1, keepdims=True)
        vn = vc * lax.rsqrt(jnp.mean(vc * vc, axis=-1, keepdims=True) + EPS) * lng_ref[...] + lnb_ref[...]
        vn = vn.astype(BF16)
        for g in range(GM_GROUPS):
            cols = slice(g * group_dim, (g + 1) * group_dim)
            mixed = jnp.dot(mix_w[g], vn[:, cols], preferred_element_type=F32) + bst[:, g:g + 1]
            gated_ref[rows, cols] = (u_ref[rows, cols].astype(F32) * mixed).astype(gated_ref.dtype)

    for slab in range(n_blocks // GM_OUT_SLAB_BLOCKS):
        for nb in range(slab * GM_OUT_SLAB_BLOCKS, (slab + 1) * GM_OUT_SLAB_BLOCKS):
            gate_block(nb)
        rows = slice(slab * GM_OUT_SLAB_BLOCKS * GM_BLOCK, (slab + 1) * GM_OUT_SLAB_BLOCKS * GM_BLOCK)
        o_ref[rows, :] = x_ref[rows, :] + jnp.dot(gated_ref[rows, :], w_ref[...], preferred_element_type=F32)


def _gm_out(z, ln_g, ln_b, w_s, b_s_t, w_out, layer, x):
    m, two_w = z.shape
    width = two_w // 2
    n = w_out.shape[2]
    tm = _tile(m, 512)
    assert tm % GM_BLOCK == 0 and width % (GM_GROUPS * LANES) == 0
    kern = functools.partial(_gm_out_kernel, n_blocks=tm // GM_BLOCK, group_dim=width // GM_GROUPS)
    return pl.pallas_call(
        kern,
        grid=(m // tm,),
        in_specs=[
            pl.BlockSpec((tm, width), lambda i: (i, 0)),
            pl.BlockSpec((tm, width), lambda i: (i, 1)),
            pl.BlockSpec((1, width), lambda i: (0, 0)),
            pl.BlockSpec((1, width), lambda i: (0, 0)),
            pl.BlockSpec((GM_GROUPS, GM_BLOCK, GM_BLOCK), lambda i: (0, 0, 0)),
            pl.BlockSpec((GM_BLOCK, GM_GROUPS), lambda i: (0, 0)),
            pl.BlockSpec((None, width, n), lambda i: (layer, 0, 0)),
            pl.BlockSpec((tm, n), lambda i: (i, 0)),
        ],
        out_specs=pl.BlockSpec((tm, n), lambda i: (i, 0)),
        out_shape=jax.ShapeDtypeStruct((m, n), F32),
        scratch_shapes=[pltpu.VMEM((tm, width), BF16)],
        compiler_params=_params(("parallel",)),
        name="gm_out",
    )(z, z, ln_g, ln_b, w_s, b_s_t, w_out, x)


def _mm_residual_kernel(a_ref, w_ref, x_ref, o_ref):
    o_ref[...] = x_ref[...] + jnp.dot(a_ref[...], w_ref[...], preferred_element_type=F32)


def _mm_residual(a, w, layer, x):
    m, k = a.shape
    n = w.shape[2]
    tm = _tile(m, 512)
    return pl.pallas_call(
        _mm_residual_kernel,
        grid=(m // tm,),
        in_specs=[
            pl.BlockSpec((tm, k), lambda i: (i, 0)),
            pl.BlockSpec((None, k, n), lambda i: (layer, 0, 0)),
            pl.BlockSpec((tm, n), lambda i: (i, 0)),
        ],
        out_specs=pl.BlockSpec((tm, n), lambda i: (i, 0)),
        out_shape=jax.ShapeDtypeStruct((m, n), F32),
        compiler_params=_params(("parallel",)),
        name="mm_residual",
    )(a, w, x)


def _gla_w_tile(j, n_tiles):
    return jnp.where(j == 0, 0, jnp.where(j == n_tiles - 1, 1, j + 1))


def _gla_p_tile(j, n_tiles):
    return jnp.where(j == 0, n_tiles - 2, jnp.minimum(j - 1, n_tiles - 3))


def _gla_in_kernel(x_ref, g_ref, w_ref, wa1_ref, wa2_ref, ba_ref, wu32_ref, wd32_ref,
                   p_ref, kt_ref, dec_ref, wu16_ref, wd16_ref, hn_ref, gs_ref, *, n_vtiles, n_tiles):
    j = pl.program_id(1)
    last = n_tiles - 1
    _cast_ffn_weight_slices(wu32_ref, wd32_ref, wu16_ref, wd16_ref)

    def project(hn):
        return jnp.dot(hn, w_ref[...], preferred_element_type=F32)

    @pl.when(j == 0)
    def _():
        hn = _rms_norm(x_ref[...], g_ref[...]).astype(BF16)
        hn_ref[...] = hn
        p_ref[...] = project(hn).astype(p_ref.dtype)
        low = jnp.dot(hn, wa1_ref[...], preferred_element_type=F32)
        gs_ref[...] = jnp.dot(low.astype(BF16), wa2_ref[...], preferred_element_type=F32) + ba_ref[...]

    tm, kd = gs_ref.shape
    tn = p_ref.shape[1]
    cs = tn // GLA_PROJ_SLABS
    sub = SUBLANES

    def gate_piece(c, l):
        rows = slice(c * CHUNK, (c + 1) * CHUNK)
        cols = slice(l * GLA_GATE_LANES, (l + 1) * GLA_GATE_LANES)
        g = _log_sigmoid(gs_ref[rows, cols]) * (1.0 / GLA_GATE_TAU)
        g3 = g.reshape(CHUNK // sub, sub, GLA_GATE_LANES)
        pos = lax.broadcasted_iota(jnp.int32, g3.shape, 1)
        shift = 1
        while shift < sub:
            g3 = g3 + jnp.where(pos >= shift, pltpu.roll(g3, shift, axis=1), 0.0)
            shift *= 2
        groups = [g3[0]]
        for r in range(1, CHUNK // sub):
            groups.append(g3[r] + groups[r - 1][sub - 1:sub, :])
        g_end = groups[-1][sub - 1:sub, :]
        gs_ref[rows, cols] = jnp.concatenate([jnp.exp(g_end - gr) for gr in groups], axis=0)
        dec_ref[c:c + 1, cols] = jnp.exp(g_end)

    pieces = [(c, l) for c in range(tm // CHUNK) for l in range(kd // GLA_GATE_LANES)]
    n_gate_steps = n_tiles - 2
    per_slot = -(-len(pieces) // (n_gate_steps * GLA_PROJ_SLABS))

    def project_with_gate_pieces(step, epilogue):
        for s in range(GLA_PROJ_SLABS):
            slot = (step - 1) * GLA_PROJ_SLABS + s
            for c, l in pieces[slot * per_slot:(slot + 1) * per_slot]:
                gate_piece(c, l)
            cols = slice(s * cs, (s + 1) * cs)
            acc = jnp.dot(hn_ref[...], w_ref[:, cols], preferred_element_type=F32)
            p_ref[:, cols] = epilogue(acc).astype(p_ref.dtype)

    for step in range(1, n_tiles - 1):
        is_value = step <= n_vtiles
        pl.when(j == step)(functools.partial(
            project_with_gate_pieces, step,
            (lambda a: a) if is_value else (lambda r: r * jax.nn.sigmoid(r))))

    @pl.when(j == last)
    def _():
        kt_ref[...] = (project(hn_ref[...]) * gs_ref[...]).T.astype(kt_ref.dtype)


def _gla_in(x, g, w, layer, wa1, wa2, ba, vd, ffn_w_up, ffn_w_down, ffn_layer):
    m, d = x.shape
    _, n_tiles, _, tn = w.shape
    n = n_tiles * tn
    kd = wa2.shape[1]
    tm = _tile(m, 1024)
    assert tn == kd
    n_vtiles = vd // tn
    assert n == 2 * kd + 2 * vd and vd % tn == 0 and n_vtiles >= 2
    assert tm % CHUNK == 0 and (tm // CHUNK) % 8 == 0 and tm % LANES == 0
    rank_pad = wa1.shape[1]
    assert tn % (GLA_PROJ_SLABS * LANES) == 0 and kd % GLA_GATE_LANES == 0 and CHUNK % SUBLANES == 0
    kern = functools.partial(_gla_in_kernel, n_vtiles=n_vtiles, n_tiles=n_tiles)
    cast_steps = n_tiles - 2
    cast_in, cast_out, cast_shape = _ffn_weight_cast_specs(
        ffn_w_up, ffn_w_down, ffn_layer, (m // tm) * cast_steps,
        lambda i, j: i * cast_steps + jnp.minimum(j, cast_steps - 1))
    return pl.pallas_call(
        kern,
        grid=(m // tm, n_tiles),
        in_specs=[
            pl.BlockSpec((tm, d), lambda i, j: (i, 0)),
            pl.BlockSpec((1, d), lambda i, j: (0, 0)),
            pl.BlockSpec((None, None, d, tn), lambda i, j: (layer, _gla_w_tile(j, n_tiles), 0, 0)),
            pl.BlockSpec((d, rank_pad), lambda i, j: (0, 0)),
            pl.BlockSpec((rank_pad, kd), lambda i, j: (0, 0)),
            pl.BlockSpec((1, kd), lambda i, j: (0, 0)),
        ] + cast_in,
        out_specs=[
            pl.BlockSpec((tm, tn), lambda i, j: (i, _gla_p_tile(j, n_tiles))),
            pl.BlockSpec((kd, tm), lambda i, j: (0, i)),
            pl.BlockSpec((tm // CHUNK, kd), lambda i, j: (i, 0)),
        ] + cast_out,
        out_shape=[
            jax.ShapeDtypeStruct((m, n - kd), BF16),
            jax.ShapeDtypeStruct((kd, m), BF16),
            jax.ShapeDtypeStruct((m // CHUNK, kd), F32),
        ] + cast_shape,
        scratch_shapes=[pltpu.VMEM((tm, d), BF16), pltpu.VMEM((tm, kd), F32)],
        compiler_params=_params(("parallel", "arbitrary")),
        name="gla_in",
    )(x, g, w, wa1, wa2, ba, ffn_w_up, ffn_w_down)


def _gla_scan_kernel(q_ref, kt_ref, v_ref, sr_ref, dec_ref, ng_ref, o_ref, s_ref, sb_ref, *, n_chunks, hk, hv):
    @pl.when(pl.program_id(1) == 0)
    def _():
        s_ref[...] = jnp.zeros_like(s_ref)

    ng = ng_ref[...]
    norm_eps = EPS * hk
    pair = 2 * CHUNK
    first_half = lax.broadcasted_iota(jnp.int32, (hk, pair), 1) < CHUNK
    heads = range(GLA_HEADS)
    kcols = [slice(h * hk, (h + 1) * hk) for h in heads]
    vcols = [slice(h * hv, (h + 1) * hv) for h in heads]

    def advance_state(c):
        prows = slice((c // 2) * pair, (c // 2 + 1) * pair)
        keep = first_half if c % 2 == 0 else jnp.logical_not(first_half)
        for h in heads:
            kt = kt_ref[kcols[h], prows]
            kt = jnp.where(keep, kt, jnp.zeros_like(kt))
            kv = jnp.dot(kt, v_ref[prows, vcols[h]], preferred_element_type=F32)
            if c == 0:
                s = dec_ref[0, kcols[h], 0:1] * s_ref[h] + kv
            else:
                s = s_ref[h] + kv
            sb_ref[c % 2, h] = s.astype(BF16)
            s_ref[h] = s * dec_ref[0, kcols[h], c + 1:c + 2] if c + 1 < n_chunks else s

    def emit_outputs(c):
        rows = slice(c * CHUNK, (c + 1) * CHUNK)
        outs = [jnp.dot(q_ref[rows, kcols[h]], sb_ref[c % 2, h], preferred_element_type=F32) for h in heads]
        for h in heads:
            o = outs[h]
            o = o * lax.rsqrt(jnp.mean(o * o, axis=-1, keepdims=True) + norm_eps) * ng[:, vcols[h]]
            o_ref[rows, vcols[h]] = (o * sr_ref[rows, vcols[h]].astype(F32)).astype(o_ref.dtype)

    advance_state(0)
    for c in range(n_chunks):
        if c + 1 < n_chunks:
            advance_state(c + 1)
        emit_outputs(c)


def _gla_scan(p, kt, dec_t, norm_g, batch, seq, kd, vd):
    m = p.shape[0]
    hk, hv = kd // GLA_HEADS, vd // GLA_HEADS
    ts = _tile(seq, 512)
    n_chunks = ts // CHUNK
    steps = seq // ts
    assert hk % LANES == 0 and hv % LANES == 0 and vd % kd == 0 and n_chunks % 2 == 0
    q_blk = 2 * vd // kd
    kern = functools.partial(_gla_scan_kernel, n_chunks=n_chunks, hk=hk, hv=hv)
    return pl.pallas_call(
        kern,
        grid=(batch, steps),
        in_specs=[
            pl.BlockSpec((ts, kd), lambda b, s: (b * steps + s, q_blk)),
            pl.BlockSpec((kd, ts), lambda b, s: (0, b * steps + s)),
            pl.BlockSpec((ts, vd), lambda b, s: (b * steps + s, 0)),
            pl.BlockSpec((ts, vd), lambda b, s: (b * steps + s, 1)),
            pl.BlockSpec((1, kd, n_chunks), lambda b, s: (b * steps + s, 0, 0)),
            pl.BlockSpec((1, vd), lambda b, s: (0, 0)),
        ],
        out_specs=pl.BlockSpec((ts, vd), lambda b, s: (b * steps + s, 0)),
        out_shape=jax.ShapeDtypeStruct((m, vd), BF16),
        scratch_shapes=[pltpu.VMEM((GLA_HEADS, hk, hv), F32), pltpu.VMEM((2, GLA_HEADS, hk, hv), BF16)],
        compiler_params=_params(("parallel", "arbitrary")),
        name="gla_scan",
    )(p, kt, p, p, dec_t, norm_g)


def _ffn_kernel(x_hbm, g_ref, wu_ref, wd_ref, fg_ref, o_ref, hn_ref, xs_ref, x_sem, *, final_norm):
    i, j = pl.program_id(0), pl.program_id(1)
    n_row_blocks = pl.num_programs(0)
    last = pl.num_programs(1) - 1
    tm = xs_ref.shape[0]

    def x_copy(row_block):
        rows = pl.ds(pl.multiple_of(row_block * tm, tm), tm)
        return pltpu.make_async_copy(x_hbm.at[rows, :], xs_ref, x_sem)

    def mlp(hn):
        a = jnp.maximum(jnp.dot(hn, wu_ref[...], preferred_element_type=F32), 0.0)
        return jnp.dot((a * a).astype(BF16), wd_ref[...], preferred_element_type=F32)

    @pl.when(j == 0)
    def _():
        @pl.when(i == 0)
        def _():
            x_copy(i).start()

        x_copy(i).wait()
        x = xs_ref[...]
        hn = _rms_norm(x, g_ref[...]).astype(BF16)
        hn_ref[...] = hn
        o_ref[...] = x + mlp(hn)

    @pl.when((j == 1) & (i + 1 < n_row_blocks))
    def _():
        x_copy(i + 1).start()

    if final_norm:
        @pl.when((j > 0) & (j < last))
        def _():
            o_ref[...] += mlp(hn_ref[...])

        @pl.when(j == last)
        def _():
            o_ref[...] = _rms_norm(o_ref[...] + mlp(hn_ref[...]), fg_ref[...])
    else:
        @pl.when(j > 0)
        def _():
            o_ref[...] += mlp(hn_ref[...])


def _ffn(x, g, w_up, w_down, final_g, final_norm):
    m, d = x.shape
    n_col, _, tf = w_up.shape
    f = n_col * tf
    tm = _tile(m, 1024)
    assert tf == _tile(f, FFN_COLS)
    assert f // tf >= 3
    kern = functools.partial(_ffn_kernel, final_norm=final_norm)
    return pl.pallas_call(
        kern,
        grid=(m // tm, f // tf),
        in_specs=[
            pl.BlockSpec(memory_space=pl.ANY),
            pl.BlockSpec((1, d), lambda i, j: (0, 0)),
            pl.BlockSpec((None, d, tf), lambda i, j: (j, 0, 0)),
            pl.BlockSpec((tf, d), lambda i, j: (j, 0)),
            pl.BlockSpec((1, d), lambda i, j: (0, 0)),
        ],
        out_specs=pl.BlockSpec((tm, d), lambda i, j: (i, 0)),
        out_shape=jax.ShapeDtypeStruct((m, d), F32),
        scratch_shapes=[pltpu.VMEM((tm, d), BF16), pltpu.VMEM((tm, d), F32), pltpu.SemaphoreType.DMA(())],
        compiler_params=_params(("arbitrary", "arbitrary")),
        name="ffn",
    )(x, g, w_up, w_down, final_g)


def _column_tile_major(w, tn):
    layers, d, n = w.shape
    return w.reshape(layers, d, n // tn, tn).transpose(0, 2, 1, 3).astype(BF16)


def kernel(x, norm_mix_g, norm_ffn_g, final_g, gm_w_in, gm_ln_g, gm_ln_b, gm_w_s, gm_b_s, gm_w_out,
           gla_w_in, gla_w_a1, gla_w_a2, gla_b_a, gla_norm_g, gla_w_o, ffn_w_up, ffn_w_down):
    batch, seq, d = x.shape
    depth = norm_mix_g.shape[0]
    m = batch * seq
    h = x.reshape(m, d)
    final_g2 = final_g.reshape(1, d)

    gm_w_out, gla_w_o = gm_w_out.astype(BF16), gla_w_o.astype(BF16)
    gm_w_in = _column_tile_major(gm_w_in, _tile(gm_w_in.shape[2], GM_MIX_COLS))
    gla_w_in = _column_tile_major(gla_w_in, gla_w_a2.shape[2])

    for i in range(depth):
        jdx = i // 2
        mix_g = norm_mix_g[i].reshape(1, d)
        if i % 2 == 0:
            z, w_up, w_down = _gm_in(h, mix_g, gm_w_in, jdx, ffn_w_up, ffn_w_down, i)
            h = _gm_out(z, gm_ln_g[jdx].reshape(1, -1), gm_ln_b[jdx].reshape(1, -1),
                        gm_w_s[jdx], gm_b_s[jdx].T, gm_w_out, jdx, h)
        else:
            kd = gla_w_a2.shape[2]
            vd = gla_w_o.shape[1]
            rank = gla_w_a1.shape[2]
            wa1 = jnp.pad(gla_w_a1[jdx], ((0, 0), (0, LANES - rank))).astype(BF16)
            wa2 = jnp.pad(gla_w_a2[jdx], ((0, LANES - rank), (0, 0))).astype(BF16)
            p, kt, dec, w_up, w_down = _gla_in(h, mix_g, gla_w_in, jdx, wa1, wa2, gla_b_a[jdx].reshape(1, kd), vd,
                                               ffn_w_up, ffn_w_down, i)
            ts = _tile(seq, 512)
            dec_t = dec.reshape(m // ts, ts // CHUNK, kd).transpose(0, 2, 1)
            o = _gla_scan(p, kt, dec_t, gla_norm_g[jdx].reshape(1, vd), batch, seq, kd, vd)
            h = _mm_residual(o, gla_w_o, jdx, h)
        h = _ffn(h, norm_ffn_g[i].reshape(1, d), w_up, w_down, final_g2, final_norm=(i == depth - 1))
    return h.reshape(batch, seq, d)
```
